```python
import jax, jax.numpy as jnp
from jax import lax
import numpy as np

D_MODEL = 1024
BATCH = 8
SEQ = 2048
DEPTH = 2

CHUNK = 64
D_MIX = D_MODEL
HEAD_DIM = 64
SB_HEADS = D_MIX // 2 // HEAD_DIM
SB_WIDTH = SB_HEADS * HEAD_DIM
SC_GROUPS = D_MIX // 4 // HEAD_DIM
SC_WIDTH = SC_GROUPS * HEAD_DIM
SG_HEADS = D_MIX // 4 // HEAD_DIM
SG_WIDTH = SG_HEADS * HEAD_DIM
SB_BLOCK = 128
SC_KERNEL = 3
SG_CHUNK = 128
D_FF = 4 * D_MODEL
PROJ_WIDTH = 3 * SB_WIDTH + 3 * SC_WIDTH + 2 * SG_WIDTH
N_MOD = 6
EPS = 1e-6

kernel_name = "hybrid_sb_conv_gmlp_adaln_trunk"


def rmsnorm(x, g):
    x32 = x.astype(jnp.float32)
    y = x32 * lax.rsqrt(jnp.mean(x32 * x32, axis=-1, keepdims=True) + EPS)
    return (y * g.astype(jnp.float32)).astype(x.dtype)


def stick_breaking_attention(q, k, v):
    bsz, seq, nh, hd = q.shape
    scale = hd ** -0.5
    q32 = q.astype(jnp.float32)
    k32 = k.astype(jnp.float32)
    v32 = v.astype(jnp.float32)
    outs = []
    for i in range(seq // SB_BLOCK):
        kv_len = (i + 1) * SB_BLOCK
        q_blk = q32[:, i * SB_BLOCK:(i + 1) * SB_BLOCK]
        k_ctx = k32[:, :kv_len]
        v_ctx = v32[:, :kv_len]
        z = jnp.einsum('bqhd,bkhd->bhqk', q_blk, k_ctx) * scale
        t_pos = i * SB_BLOCK + jnp.arange(SB_BLOCK)
        s_pos = jnp.arange(kv_len)
        mask = s_pos[None, :] < t_pos[:, None]
        log_beta = jax.nn.log_sigmoid(z)
        log_stay = jnp.where(mask, jax.nn.log_sigmoid(-z), 0.0)
        log_after = lax.cumsum(log_stay, axis=3, reverse=True) - log_stay
        w = jnp.where(mask, jnp.exp(log_beta + log_after), 0.0)
        outs.append(jnp.einsum('bhqk,bkhd->bqhd', w, v_ctx))
    return jnp.concatenate(outs, axis=1).astype(q.dtype)


def short_gated_conv(b_gate, c_gate, h, conv_w, conv_b):
    u = c_gate * h
    y = lax.conv_general_dilated(
        u, conv_w[:, None, :].astype(u.dtype), window_strides=(1,),
        padding=[(SC_KERNEL - 1, 0)], dimension_numbers=('NWC', 'WIO', 'NWC'),
        feature_group_count=SC_WIDTH)
    return b_gate * (y + conv_b)


def spatial_gating(u, v, norm_g, sw, sb):
    bsz, seq, _ = v.shape
    u = jax.nn.gelu(u)
    v = rmsnorm(jax.nn.gelu(v), norm_g)
    n_win = seq // SG_CHUNK
    v = v.reshape(bsz, n_win, SG_CHUNK, SG_HEADS, HEAD_DIM)
    chunk_id = jnp.arange(SG_CHUNK) // CHUNK
    mask = chunk_id[:, None] >= chunk_id[None, :]
    w = jnp.where(mask[None], sw, jnp.zeros_like(sw))
    mixed = jnp.einsum('hts,bnshd->bnthd', w, v) + sb.T[None, None, :, :, None]
    return u * mixed.reshape(bsz, seq, SG_WIDTH)


def setup_inputs(seed: int = 0) -> dict:
    key = jax.random.key(seed)
    ks = jax.random.split(key, 16)
    f32 = jnp.float32
    L = DEPTH
    x = jax.random.normal(ks[0], (BATCH, SEQ, D_MODEL), f32)
    c = jax.random.normal(ks[1], (BATCH, D_MODEL), f32)
    ada_w = jax.random.normal(ks[2], (L, D_MODEL, N_MOD * D_MODEL), f32) * D_MODEL ** -0.5
    ada_b = jax.random.normal(ks[3], (L, N_MOD * D_MODEL), f32) * 0.02
    norm_mix_g = 1.0 + 0.02 * jax.random.normal(ks[4], (L, D_MODEL), f32)
    norm_mlp_g = 1.0 + 0.02 * jax.random.normal(ks[5], (L, D_MODEL), f32)
    w_in = jax.random.normal(ks[6], (L, D_MODEL, PROJ_WIDTH), f32) * D_MODEL ** -0.5
    conv_w = jax.random.normal(ks[7], (L, SC_KERNEL, SC_WIDTH), f32) * SC_KERNEL ** -0.5
    conv_b = jax.random.normal(ks[8], (L, SC_WIDTH), f32) * 0.02
    gmlp_norm_g = 1.0 + 0.02 * jax.random.normal(ks[9], (L, SG_WIDTH), f32)
    spatial_w = jax.random.normal(ks[10], (L, SG_HEADS, SG_CHUNK, SG_CHUNK), f32) * SG_CHUNK ** -0.5
    spatial_b = 1.0 + 0.1 * jax.random.normal(ks[11], (L, SG_HEADS, SG_CHUNK), f32)
    w_out = jax.random.normal(ks[12], (L, D_MIX, D_MODEL), f32) * D_MIX ** -0.5
    mlp_w1 = jax.random.normal(ks[13], (L, D_MODEL, D_FF), f32) * D_MODEL ** -0.5
    mlp_w2 = jax.random.normal(ks[14], (L, D_FF, D_MODEL), f32) * D_FF ** -0.5
    final_norm_g = 1.0 + 0.02 * jax.random.normal(ks[15], (D_MODEL,), f32)
    return {"x": x, "c": c, "ada_w": ada_w, "ada_b": ada_b,
            "norm_mix_g": norm_mix_g, "norm_mlp_g": norm_mlp_g, "w_in": w_in,
            "conv_w": conv_w, "conv_b": conv_b, "gmlp_norm_g": gmlp_norm_g,
            "spatial_w": spatial_w, "spatial_b": spatial_b, "w_out": w_out,
            "mlp_w1": mlp_w1, "mlp_w2": mlp_w2, "final_norm_g": final_norm_g}


def reference(x, c, ada_w, ada_b, norm_mix_g, norm_mlp_g, w_in, conv_w, conv_b,
              gmlp_norm_g, spatial_w, spatial_b, w_out, mlp_w1, mlp_w2, final_norm_g):
    bsz, seq, _ = x.shape
    bounds = [SB_WIDTH, 2 * SB_WIDTH, 3 * SB_WIDTH,
              3 * SB_WIDTH + SC_WIDTH, 3 * SB_WIDTH + 2 * SC_WIDTH,
              3 * SB_WIDTH + 3 * SC_WIDTH, 3 * SB_WIDTH + 3 * SC_WIDTH + SG_WIDTH]
    c_act = jax.nn.silu(c)
    for l in range(DEPTH):
        mod = c_act @ ada_w[l] + ada_b[l]
        sh_m, sc_m, g_m, sh_f, sc_f, g_f = jnp.split(mod[:, None, :], N_MOD, axis=-1)

        h = rmsnorm(x, norm_mix_g[l]) * (1.0 + sc_m) + sh_m
        proj = h @ w_in[l]
        q, k, v, b_gate, c_gate, h_conv, u_sg, v_sg = jnp.split(proj, bounds, axis=-1)
        a_out = stick_breaking_attention(
            q.reshape(bsz, seq, SB_HEADS, HEAD_DIM),
            k.reshape(bsz, seq, SB_HEADS, HEAD_DIM),
            v.reshape(bsz, seq, SB_HEADS, HEAD_DIM)).reshape(bsz, seq, SB_WIDTH)
        c_out = short_gated_conv(b_gate, c_gate, h_conv, conv_w[l], conv_b[l])
        s_out = spatial_gating(u_sg, v_sg, gmlp_norm_g[l], spatial_w[l], spatial_b[l])
        mix = jnp.concatenate([a_out, c_out, s_out], axis=-1) @ w_out[l]
        x = x + g_m * mix

        h = rmsnorm(x, norm_mlp_g[l]) * (1.0 + sc_f) + sh_f
        x = x + g_f * (jnp.square(jax.nn.relu(h @ mlp_w1[l])) @ mlp_w2[l])
    return rmsnorm(x, final_norm_g)
```

```python
import functools
import math

import jax
import jax.numpy as jnp
from jax import lax
from jax.experimental import pallas as pl
from jax.experimental.pallas import tpu as pltpu

F32 = jnp.float32
BF16 = jnp.bfloat16

HEAD_DIM = 64
SB_WIDTH = 512
SC_WIDTH = 256
SG_WIDTH = 256
SG_HEADS = 4
SG_CHUNK = 128
CHUNK = 64
N_MOD = 6
EPS = 1e-6
LANES = 128
BF16_SUBLANES = 16
LOG2E = 1.4426950408889634

VMEM_LIMIT = 48 * 1024 * 1024


def _params(sem):
    return pltpu.CompilerParams(dimension_semantics=sem, vmem_limit_bytes=VMEM_LIMIT)


def _adaln_kernel(c_ref, w_ref, b_ref, o_ref):
    c = c_ref[...]
    c_act = (c * jax.nn.sigmoid(c)).astype(BF16)
    o_ref[...] = jnp.dot(c_act, w_ref[...].astype(BF16), preferred_element_type=F32) + b_ref[...]


def _adaln(c, ada_w, ada_b):
    depth, d, width = ada_w.shape
    bsz = c.shape[0]
    tn = 1536
    return pl.pallas_call(
        _adaln_kernel,
        grid=(depth, width // tn),
        in_specs=[
            pl.BlockSpec((bsz, d), lambda l, j: (0, 0)),
            pl.BlockSpec((None, d, tn), lambda l, j: (l, 0, j)),
            pl.BlockSpec((None, 1, tn), lambda l, j: (l, 0, j)),
        ],
        out_specs=pl.BlockSpec((None, bsz, tn), lambda l, j: (l, 0, j)),
        out_shape=jax.ShapeDtypeStruct((depth, bsz, width), F32),
        compiler_params=_params(("parallel", "parallel")),
        name="adaln",
    )(c, ada_w, ada_b.reshape(depth, 1, width))


def _modulated_norm(x, g, sc, sh):
    r = lax.rsqrt(jnp.mean(x * x, axis=-1, keepdims=True) + EPS)
    return x * r * (g * (1.0 + sc)) + sh


def _norm_proj_kernel(x_ref, g_ref, sc_ref, sh_ref, w_ref, o_ref, *, col_chunks):
    h = _modulated_norm(x_ref[...], g_ref[...], sc_ref[...], sh_ref[...]).astype(BF16)
    for lo, hi in col_chunks:
        o_ref[:, lo:hi] = jnp.dot(h, w_ref[:, lo:hi], preferred_element_type=F32).astype(BF16)


def _norm_proj(x, norm_g, mod, w_in, layer, seq):
    n, d = x.shape
    width = w_in.shape[-1]
    bsz = n // seq
    tm = 512
    per_seq = seq // tm
    tn = 512
    col_chunks = tuple((lo, min(lo + tn, width)) for lo in range(0, width, tn))

    def mod_idx(k):
        return lambda i: ((layer * bsz + i // per_seq) * N_MOD + k, 0, 0)

    return pl.pallas_call(
        functools.partial(_norm_proj_kernel, col_chunks=col_chunks),
        grid=(n // tm,),
        in_specs=[
            pl.BlockSpec((tm, d), lambda i: (i, 0)),
            pl.BlockSpec((None, 1, d), lambda i: (layer, 0, 0)),
            pl.BlockSpec((None, 1, d), mod_idx(1)),
            pl.BlockSpec((None, 1, d), mod_idx(0)),
            pl.BlockSpec((None, d, width), lambda i: (layer, 0, 0)),
        ],
        out_specs=pl.BlockSpec((tm, width), lambda i: (i, 0)),
        out_shape=jax.ShapeDtypeStruct((n, width), BF16),
        compiler_params=_params(("parallel",)),
        name="norm_proj",
    )(x, norm_g, mod, mod, w_in)


def _attn_kernel(q_ref, k_ref, v_ref, o_ref, *, blk):
    qi = pl.program_id(2)
    lane = lax.broadcasted_iota(jnp.int32, (blk, LANES), 1)
    head0 = lane < HEAD_DIM
    qscale = HEAD_DIM ** -0.5 * LOG2E
    q = q_ref[...].astype(F32)
    qq = jnp.concatenate([jnp.where(head0, q * qscale, 0.0),
                          jnp.where(head0, 0.0, q * qscale)], axis=0).astype(BF16)
    vmask0 = jnp.where(head0, 1.0, 0.0).astype(BF16)
    vmask1 = jnp.where(head0, 0.0, 1.0).astype(BF16)

    r = lax.broadcasted_iota(jnp.int32, (2 * blk, 2 * blk), 0)
    c = lax.broadcasted_iota(jnp.int32, (2 * blk, 2 * blk), 1)
    cum = jnp.where((c >= blk) | ((r & (blk - 1)) > c), 1.0, 0.0).astype(BF16)

    t_row = lax.broadcasted_iota(jnp.int32, (2 * blk, blk), 0) & (blk - 1)
    s_col = lax.broadcasted_iota(jnp.int32, (2 * blk, blk), 1)
    causal = s_col < t_row

    def step(j, carry, acc, diagonal):
        start = pl.multiple_of(j * blk, blk)
        kb = k_ref[pl.ds(start, blk), :]
        vb = v_ref[pl.ds(start, blk), :]
        z = lax.dot_general(qq, kb, (((1,), (1,)), ((), ())), preferred_element_type=F32)
        e = jnp.exp2(-jnp.abs(z))
        sp = jnp.log(1.0 + e) * LOG2E
        log_beta = jnp.minimum(z, 0.0) - sp
        log_stay = log_beta - z
        if diagonal:
            log_stay = jnp.where(causal, log_stay, 0.0)
        hi = log_stay.astype(BF16)
        lo = (log_stay - hi.astype(F32)).astype(BF16)
        sums = jnp.dot(jnp.concatenate([hi, lo], axis=1), cum, preferred_element_type=F32)
        w = jnp.exp2(log_beta + sums[:, :blk] + carry)
        if diagonal:
            w = jnp.where(causal, w, 0.0)
        w = w.astype(BF16)
        w2 = jnp.concatenate([w[:blk], w[blk:]], axis=1)
        vv = jnp.concatenate([vb * vmask0, vb * vmask1], axis=0)
        acc = acc + jnp.dot(w2, vv, preferred_element_type=F32)
        return carry + sums[:, blk:], acc

    carry, acc = step(qi, jnp.zeros((2 * blk, blk), F32), jnp.zeros((blk, LANES), F32), True)

    def body(jj, state):
        return step(qi - 1 - jj, state[0], state[1], False)

    carry, acc = lax.fori_loop(0, qi, body, (carry, acc))
    o_ref[...] = acc.astype(BF16)


def _attention(proj, bsz, seq):
    width = proj.shape[-1]
    proj3 = proj.reshape(bsz, seq, width)
    blk = 128
    pairs = SB_WIDTH // LANES
    return pl.pallas_call(
        functools.partial(_attn_kernel, blk=blk),
        grid=(bsz, pairs, seq // blk),
        in_specs=[
            pl.BlockSpec((None, blk, LANES), lambda b, h, i: (b, i, h)),
            pl.BlockSpec((None, seq, LANES), lambda b, h, i: (b, 0, pairs + h)),
            pl.BlockSpec((None, seq, LANES), lambda b, h, i: (b, 0, 2 * pairs + h)),
        ],
        out_specs=pl.BlockSpec((None, blk, LANES), lambda b, h, i: (b, i, h)),
        out_shape=jax.ShapeDtypeStruct((bsz, seq, SB_WIDTH), BF16),
        compiler_params=_params(("parallel", "parallel", "arbitrary")),
        name="sb_attention",
    )(proj3, proj3, proj3).reshape(bsz * seq, SB_WIDTH)


def _gelu(x):
    return 0.5 * x * (1.0 + jnp.tanh(0.7978845608028654 * (x + 0.044715 * (x * x * x))))


def _mix_out_kernel(a_ref, bg_ref, cg_ref, hc_ref, us_ref, vs_ref, cgh_ref, hch_ref,
                    x_ref, gm_ref, cw_ref, cb_ref, ng_ref, sw_ref, sbias_ref, wo_ref,
                    o_ref, uext_ref, *, tm, per_seq, halo):
    i = pl.program_id(0)

    u = cg_ref[...].astype(F32) * hc_ref[...].astype(F32)
    u_halo = cgh_ref[...].astype(F32) * hch_ref[...].astype(F32)
    u_halo = jnp.where(i % per_seq == 0, 0.0, u_halo)
    uext_ref[0:halo, :] = u_halo
    uext_ref[halo:halo + tm, :] = u
    u_m1 = uext_ref[halo - 1:halo - 1 + tm, :]
    u_m2 = uext_ref[halo - 2:halo - 2 + tm, :]
    cw = cw_ref[...]
    y = cw[0:1] * u_m2 + cw[1:2] * u_m1 + cw[2:3] * u + cb_ref[...]
    c_out = bg_ref[...].astype(F32) * y

    ug = _gelu(us_ref[...].astype(F32))
    vg = _gelu(vs_ref[...].astype(F32))
    vn = vg * lax.rsqrt(jnp.mean(vg * vg, axis=-1, keepdims=True) + EPS) * ng_ref[...]
    t_chunk = lax.broadcasted_iota(jnp.int32, (SG_CHUNK, SG_CHUNK), 0) // CHUNK
    s_chunk = lax.broadcasted_iota(jnp.int32, (SG_CHUNK, SG_CHUNK), 1) // CHUNK
    chunk_causal = t_chunk >= s_chunk
    w_cat = jnp.concatenate(
        [jnp.where(chunk_causal, sw_ref[h], 0.0) for h in range(SG_HEADS)], axis=1).astype(BF16)
    lane_head = lax.broadcasted_iota(jnp.int32, (SG_CHUNK, SG_WIDTH), 1) // HEAD_DIM
    sbias = sbias_ref[...]
    mixed = []
    for n in range(tm // SG_CHUNK):
        v_win = vn[n * SG_CHUNK:(n + 1) * SG_CHUNK]
        v_stack = jnp.concatenate(
            [jnp.where(lane_head == h, v_win, 0.0) for h in range(SG_HEADS)], axis=0).astype(BF16)
        mixed.append(jnp.dot(w_cat, v_stack, preferred_element_type=F32) + sbias)
    s_out = ug * jnp.concatenate(mixed, axis=0)

    cat = jnp.concatenate([a_ref[...], c_out.astype(BF16), s_out.astype(BF16)], axis=1)
    mix = jnp.dot(cat, wo_ref[...], preferred_element_type=F32)
    o_ref[...] = x_ref[...] + gm_ref[...] * mix


def _mix_out(a_out, proj, x, mod, conv_w, conv_b, gmlp_norm_g, spatial_w, sbias, w_out, layer, seq):
    n, d = x.shape
    bsz = n // seq
    tm = 512
    per_seq = seq // tm
    halo = BF16_SUBLANES
    col0 = 3 * SB_WIDTH // SC_WIDTH

    def col(k):
        return pl.BlockSpec((tm, SC_WIDTH), lambda i: (i, col0 + k))

    def halo_col(k):
        return pl.BlockSpec((halo, SC_WIDTH),
                            lambda i: (jnp.maximum(i * (tm // halo) - 1, 0), col0 + k))

    return pl.pallas_call(
        functools.partial(_mix_out_kernel, tm=tm, per_seq=per_seq, halo=halo),
        grid=(n // tm,),
        in_specs=[
            pl.BlockSpec((tm, SB_WIDTH), lambda i: (i, 0)),
            col(0), col(1), col(2), col(3), col(4),
            halo_col(1), halo_col(2),
            pl.BlockSpec((tm, d), lambda i: (i, 0)),
            pl.BlockSpec((None, 1, d), lambda i: ((layer * bsz + i // per_seq) * N_MOD + 2, 0, 0)),
            pl.BlockSpec((None, 3, SC_WIDTH), lambda i: (layer, 0, 0)),
            pl.BlockSpec((None, 1, SC_WIDTH), lambda i: (layer, 0, 0)),
            pl.BlockSpec((None, 1, SG_WIDTH), lambda i: (layer, 0, 0)),
            pl.BlockSpec((None, SG_HEADS, SG_CHUNK, SG_CHUNK), lambda i: (layer, 0, 0, 0)),
            pl.BlockSpec((None, SG_CHUNK, SG_WIDTH), lambda i: (layer, 0, 0)),
            pl.BlockSpec((None, d, d), lambda i: (layer, 0, 0)),
        ],
        out_specs=pl.BlockSpec((tm, d), lambda i: (i, 0)),
        out_shape=jax.ShapeDtypeStruct((n, d), F32),
        scratch_shapes=[pltpu.VMEM((halo + tm, SC_WIDTH), F32)],
        compiler_params=_params(("parallel",)),
        name="mix_out",
    )(a_out, proj, proj, proj, proj, proj, proj, proj, x, mod,
      conv_w, conv_b, gmlp_norm_g, spatial_w, sbias, w_out)


def _mlp_kernel(x_ref, g_ref, sc_ref, sh_ref, gf_ref, w1_ref, w2_ref, fg_ref, o_ref,
                h_ref, acc_ref, *, final_norm):
    j = pl.program_id(1)

    @pl.when(j == 0)
    def _():
        h_ref[...] = _modulated_norm(x_ref[...], g_ref[...], sc_ref[...], sh_ref[...]).astype(BF16)
        acc_ref[...] = jnp.zeros_like(acc_ref)

    t = jnp.dot(h_ref[...], w1_ref[...], preferred_element_type=F32)
    t = jnp.square(jnp.maximum(t, 0.0)).astype(BF16)
    acc_ref[...] += jnp.dot(t, w2_ref[...], preferred_element_type=F32)

    @pl.when(j == pl.num_programs(1) - 1)
    def _():
        y = x_ref[...] + gf_ref[...] * acc_ref[...]
        if final_norm:
            y = y * lax.rsqrt(jnp.mean(y * y, axis=-1, keepdims=True) + EPS) * fg_ref[...]
        o_ref[...] = y


def _mlp(x, norm_g, mod, w1, w2, final_g, layer, seq, final_norm):
    n, d = x.shape
    d_ff = w1.shape[-1]
    bsz = n // seq
    tm = 1024
    tf = 512
    per_seq = seq // tm

    def mod_idx(k):
        return lambda i, j: ((layer * bsz + i // per_seq) * N_MOD + k, 0, 0)

    return pl.pallas_call(
        functools.partial(_mlp_kernel, final_norm=final_norm),
        grid=(n // tm, d_ff // tf),
        in_specs=[
            pl.BlockSpec((tm, d), lambda i, j: (i, 0)),
            pl.BlockSpec((None, 1, d), lambda i, j: (layer, 0, 0)),
            pl.BlockSpec((None, 1, d), mod_idx(4)),
            pl.BlockSpec((None, 1, d), mod_idx(3)),
            pl.BlockSpec((None, 1, d), mod_idx(5)),
            pl.BlockSpec((None, d, tf), lambda i, j: (layer, 0, j)),
            pl.BlockSpec((None, tf, d), lambda i, j: (layer, j, 0)),
            pl.BlockSpec((1, d), lambda i, j: (0, 0)),
        ],
        out_specs=pl.BlockSpec((tm, d), lambda i, j: (i, 0)),
        out_shape=jax.ShapeDtypeStruct((n, d), F32),
        scratch_shapes=[pltpu.VMEM((tm, d), BF16), pltpu.VMEM((tm, d), F32)],
        compiler_params=_params(("parallel", "arbitrary")),
        name="mlp",
    )(x, norm_g, mod, mod, mod, w1, w2, final_g)


def kernel(x, c, ada_w, ada_b, norm_mix_g, norm_mlp_g, w_in, conv_w, conv_b, gmlp_norm_g,
           spatial_w, spatial_b, w_out, mlp_w1, mlp_w2, final_norm_g):
    bsz, seq, d = x.shape
    depth = ada_w.shape[0]
    xf = x.reshape(bsz * seq, d)

    mod = _adaln(c, ada_w, ada_b).reshape(depth * bsz * N_MOD, 1, d)
    w_in_b = w_in.astype(BF16)
    w_out_b = w_out.astype(BF16)
    w1_b = mlp_w1.astype(BF16)
    w2_b = mlp_w2.astype(BF16)
    norm_mix = norm_mix_g.reshape(depth, 1, d)
    norm_mlp = norm_mlp_g.reshape(depth, 1, d)
    conv_b3 = conv_b.reshape(depth, 1, SC_WIDTH)
    gmlp_g3 = gmlp_norm_g.reshape(depth, 1, SG_WIDTH)
    sbias = jnp.repeat(jnp.swapaxes(spatial_b, 1, 2), HEAD_DIM, axis=2)
    final_g = final_norm_g.reshape(1, d)

    for layer in range(depth):
        proj = _norm_proj(xf, norm_mix, mod, w_in_b, layer, seq)
        a_out = _attention(proj, bsz, seq)
        xf = _mix_out(a_out, proj, xf, mod, conv_w, conv_b3, gmlp_g3, spatial_w, sbias,
                      w_out_b, layer, seq)
        xf = _mlp(xf, norm_mlp, mod, w1_b, w2_b, final_g, layer, seq,
                  final_norm=(layer == depth - 1))
    return xf.reshape(bsz, seq, d)
```

```python
import functools
import math

import jax
import jax.numpy as jnp
from jax import lax
from jax.experimental import pallas as pl
from jax.experimental.pallas import tpu as pltpu

F32 = jnp.float32
BF16 = jnp.bfloat16

HEAD_DIM = 64
SB_WIDTH = 512
SC_WIDTH = 256
SG_WIDTH = 256
SG_HEADS = 4
SG_CHUNK = 128
CHUNK = 64
N_MOD = 6
EPS = 1e-6
LANES = 128
BF16_SUBLANES = 16
LOG2E = 1.4426950408889634

VMEM_LIMIT = 48 * 1024 * 1024


def _params(sem):
    return pltpu.CompilerParams(dimension_semantics=sem, vmem_limit_bytes=VMEM_LIMIT)


def _adaln_kernel(c_ref, w_ref, b_ref, o_ref):
    c = c_ref[...]
    c_act = (c * jax.nn.sigmoid(c)).astype(BF16)
    o_ref[...] = jnp.dot(c_act, w_ref[...].astype(BF16), preferred_element_type=F32) + b_ref[...]


def _adaln(c, ada_w, ada_b):
    depth, d, width = ada_w.shape
    bsz = c.shape[0]
    tn = 1536
    return pl.pallas_call(
        _adaln_kernel,
        grid=(depth, width // tn),
        in_specs=[
            pl.BlockSpec((bsz, d), lambda l, j: (0, 0)),
            pl.BlockSpec((None, d, tn), lambda l, j: (l, 0, j)),
            pl.BlockSpec((None, 1, tn), lambda l, j: (l, 0, j)),
        ],
        out_specs=pl.BlockSpec((None, bsz, tn), lambda l, j: (l, 0, j)),
        out_shape=jax.ShapeDtypeStruct((depth, bsz, width), F32),
        compiler_params=_params(("parallel", "parallel")),
        name="adaln",
    )(c, ada_w, ada_b.reshape(depth, 1, width))


def _modulated_norm(x, g, sc, sh):
    r = lax.rsqrt(jnp.mean(x * x, axis=-1, keepdims=True) + EPS)
    return x * r * (g * (1.0 + sc)) + sh


def _norm_proj_kernel(x_ref, g_ref, sc_ref, sh_ref, w_ref, o_ref, *, col_chunks):
    h = _modulated_norm(x_ref[...], g_ref[...], sc_ref[...], sh_ref[...]).astype(BF16)
    for lo, hi in col_chunks:
        o_ref[:, lo:hi] = jnp.dot(h, w_ref[:, lo:hi], preferred_element_type=F32).astype(BF16)


def _norm_proj(x, norm_g, mod, w_in, layer, seq):
    n, d = x.shape
    width = w_in.shape[-1]
    bsz = n // seq
    tm = 512
    per_seq = seq // tm
    tn = 512
    col_chunks = tuple((lo, min(lo + tn, width)) for lo in range(0, width, tn))

    def mod_idx(k):
        return lambda i: ((layer * bsz + i // per_seq) * N_MOD + k, 0, 0)

    return pl.pallas_call(
        functools.partial(_norm_proj_kernel, col_chunks=col_chunks),
        grid=(n // tm,),
        in_specs=[
            pl.BlockSpec((tm, d), lambda i: (i, 0)),
            pl.BlockSpec((None, 1, d), lambda i: (layer, 0, 0)),
            pl.BlockSpec((None, 1, d), mod_idx(1)),
            pl.BlockSpec((None, 1, d), mod_idx(0)),
            pl.BlockSpec((None, d, width), lambda i: (layer, 0, 0)),
        ],
        out_specs=pl.BlockSpec((tm, width), lambda i: (i, 0)),
        out_shape=jax.ShapeDtypeStruct((n, width), BF16),
        compiler_params=_params(("parallel",)),
        name="norm_proj",
    )(x, norm_g, mod, mod, w_in)


def _attn_kernel(q_ref, k_ref, v_ref, o_ref, qq_ref, carry_ref, acc_ref, *, blk, pairs):
    qi = pl.program_id(1)
    win = 2 * blk
    rows = 2 * blk
    head0 = lax.broadcasted_iota(jnp.int32, (blk, LANES), 1) < HEAD_DIM
    qscale = -(HEAD_DIM ** -0.5) * LOG2E
    for p in range(pairs):
        q = q_ref[:, p * LANES:(p + 1) * LANES].astype(F32) * qscale
        qq_ref[p] = jnp.concatenate([jnp.where(head0, q, 0.0),
                                     jnp.where(head0, 0.0, q)], axis=0).astype(BF16)

    r = lax.broadcasted_iota(jnp.int32, (2 * win, win), 0) & (win - 1)
    c = lax.broadcasted_iota(jnp.int32, (2 * win, win), 1)
    cum = jnp.where(r >= c, 1.0, 0.0).astype(BF16)

    first = qi // 2
    t_row = lax.broadcasted_iota(jnp.int32, (rows, win), 0) & (blk - 1)
    s_col = lax.broadcasted_iota(jnp.int32, (rows, win), 1)
    causal = s_col + (first * win - qi * blk) < t_row

    def window(w_idx, diagonal):
        start = pl.multiple_of(w_idx * win, win)
        ps = range(pairs)
        nz = [lax.dot_general(qq_ref[p], k_ref[pl.ds(start, win), p * LANES:(p + 1) * LANES],
                              (((1,), (1,)), ((), ())), preferred_element_type=F32) for p in ps]
        hl = []
        for p in ps:
            sp = jnp.log(1.0 + jnp.exp2(-jnp.abs(nz[p]))) * LOG2E
            log_stay = jnp.minimum(nz[p], 0.0) - sp
            if diagonal:
                log_stay = jnp.where(causal, log_stay, 0.0)
            hi = log_stay.astype(BF16)
            lo = (log_stay - hi.astype(F32)).astype(BF16)
            hl.append(jnp.concatenate([hi, lo], axis=1))
        incl = [jnp.dot(hl[p], cum, preferred_element_type=F32) for p in ps]
        wts = []
        for p in ps:
            if diagonal:
                w = jnp.where(causal, jnp.exp2(incl[p] - nz[p]), 0.0)
                carry_ref[p] = incl[p][:, 0:1]
            else:
                carry = carry_ref[p]
                w = jnp.exp2(incl[p] + carry - nz[p])
                carry_ref[p] = carry + incl[p][:, 0:1]
            wts.append(w.astype(BF16))
        for p in ps:
            pv = jnp.dot(wts[p], v_ref[pl.ds(start, win), p * LANES:(p + 1) * LANES],
                         preferred_element_type=F32)
            if diagonal:
                acc_ref[p] = pv
            else:
                acc_ref[p] += pv

    window(first, True)

    def body(ww, _):
        window(first - 1 - ww, False)
        return 0

    lax.fori_loop(0, first, body, 0)
    for p in range(pairs):
        acc = acc_ref[p]
        o_ref[:, p * LANES:(p + 1) * LANES] = jnp.where(head0, acc[:blk], acc[blk:]).astype(BF16)


def _attention(proj, bsz, seq):
    width = proj.shape[-1]
    proj3 = proj.reshape(bsz, seq, width)
    blk = 128
    pairs = SB_WIDTH // LANES
    return pl.pallas_call(
        functools.partial(_attn_kernel, blk=blk, pairs=pairs),
        grid=(bsz, seq // blk),
        in_specs=[
            pl.BlockSpec((None, blk, SB_WIDTH), lambda b, i: (b, i, 0)),
            pl.BlockSpec((None, seq, SB_WIDTH), lambda b, i: (b, 0, 1)),
            pl.BlockSpec((None, seq, SB_WIDTH), lambda b, i: (b, 0, 2)),
        ],
        out_specs=pl.BlockSpec((None, blk, SB_WIDTH), lambda b, i: (b, i, 0)),
        out_shape=jax.ShapeDtypeStruct((bsz, seq, SB_WIDTH), BF16),
        scratch_shapes=[pltpu.VMEM((pairs, 2 * blk, LANES), BF16),
                        pltpu.VMEM((pairs, 2 * blk, 1), F32),
                        pltpu.VMEM((pairs, 2 * blk, LANES), F32)],
        compiler_params=_params(("parallel", "arbitrary")),
        name="sb_attention",
    )(proj3, proj3, proj3).reshape(bsz * seq, SB_WIDTH)


def _gelu(x):
    return 0.5 * x * (1.0 + jnp.tanh(0.7978845608028654 * (x + 0.044715 * (x * x * x))))


def _mix_out_kernel(a_ref, bg_ref, cg_ref, hc_ref, us_ref, vs_ref, cgh_ref, hch_ref,
                    x_ref, gm_ref, cw_ref, cb_ref, ng_ref, sw_ref, sbias_ref, wo_ref,
                    o_ref, uext_ref, *, tm, per_seq, halo):
    i = pl.program_id(0)

    u = cg_ref[...].astype(F32) * hc_ref[...].astype(F32)
    u_halo = cgh_ref[...].astype(F32) * hch_ref[...].astype(F32)
    u_halo = jnp.where(i % per_seq == 0, 0.0, u_halo)
    uext_ref[0:halo, :] = u_halo
    uext_ref[halo:halo + tm, :] = u
    u_m1 = uext_ref[halo - 1:halo - 1 + tm, :]
    u_m2 = uext_ref[halo - 2:halo - 2 + tm, :]
    cw = cw_ref[...]
    y = cw[0:1] * u_m2 + cw[1:2] * u_m1 + cw[2:3] * u + cb_ref[...]
    c_out = bg_ref[...].astype(F32) * y

    ug = _gelu(us_ref[...].astype(F32))
    vg = _gelu(vs_ref[...].astype(F32))
    vn = vg * lax.rsqrt(jnp.mean(vg * vg, axis=-1, keepdims=True) + EPS) * ng_ref[...]
    t_chunk = lax.broadcasted_iota(jnp.int32, (SG_CHUNK, SG_CHUNK), 0) // CHUNK
    s_chunk = lax.broadcasted_iota(jnp.int32, (SG_CHUNK, SG_CHUNK), 1) // CHUNK
    chunk_causal = t_chunk >= s_chunk
    w_cat = jnp.concatenate(
        [jnp.where(chunk_causal, sw_ref[h], 0.0) for h in range(SG_HEADS)], axis=1).astype(BF16)
    lane_head = lax.broadcasted_iota(jnp.int32, (SG_CHUNK, SG_WIDTH), 1) // HEAD_DIM
    sbias = sbias_ref[...]
    mixed = []
    for n in range(tm // SG_CHUNK):
        v_win = vn[n * SG_CHUNK:(n + 1) * SG_CHUNK]
        v_stack = jnp.concatenate(
            [jnp.where(lane_head == h, v_win, 0.0) for h in range(SG_HEADS)], axis=0).astype(BF16)
        mixed.append(jnp.dot(w_cat, v_stack, preferred_element_type=F32) + sbias)
    s_out = ug * jnp.concatenate(mixed, axis=0)

    cat = jnp.concatenate([a_ref[...], c_out.astype(BF16), s_out.astype(BF16)], axis=1)
    mix = jnp.dot(cat, wo_ref[...], preferred_element_type=F32)
    o_ref[...] = x_ref[...] + gm_ref[...] * mix


def _mix_out(a_out, proj, x, mod, conv_w, conv_b, gmlp_norm_g, spatial_w, sbias, w_out, layer, seq):
    n, d = x.shape
    bsz = n // seq
    tm = 512
    per_seq = seq // tm
    halo = BF16_SUBLANES
    col0 = 3 * SB_WIDTH // SC_WIDTH

    def col(k):
        return pl.BlockSpec((tm, SC_WIDTH), lambda i: (i, col0 + k))

    def halo_col(k):
        return pl.BlockSpec((halo, SC_WIDTH),
                            lambda i: (jnp.maximum(i * (tm // halo) - 1, 0), col0 + k))

    return pl.pallas_call(
        functools.partial(_mix_out_kernel, tm=tm, per_seq=per_seq, halo=halo),
        grid=(n // tm,),
        in_specs=[
            pl.BlockSpec((tm, SB_WIDTH), lambda i: (i, 0)),
            col(0), col(1), col(2), col(3), col(4),
            halo_col(1), halo_col(2),
            pl.BlockSpec((tm, d), lambda i: (i, 0)),
            pl.BlockSpec((None, 1, d), lambda i: ((layer * bsz + i // per_seq) * N_MOD + 2, 0, 0)),
            pl.BlockSpec((None, 3, SC_WIDTH), lambda i: (layer, 0, 0)),
            pl.BlockSpec((None, 1, SC_WIDTH), lambda i: (layer, 0, 0)),
            pl.BlockSpec((None, 1, SG_WIDTH), lambda i: (layer, 0, 0)),
            pl.BlockSpec((None, SG_HEADS, SG_CHUNK, SG_CHUNK), lambda i: (layer, 0, 0, 0)),
            pl.BlockSpec((None, SG_CHUNK, SG_WIDTH), lambda i: (layer, 0, 0)),
            pl.BlockSpec((None, d, d), lambda i: (layer, 0, 0)),
        ],
        out_specs=pl.BlockSpec((tm, d), lambda i: (i, 0)),
        out_shape=jax.ShapeDtypeStruct((n, d), F32),
        scratch_shapes=[pltpu.VMEM((halo + tm, SC_WIDTH), F32)],
        compiler_params=_params(("parallel",)),
        name="mix_out",
    )(a_out, proj, proj, proj, proj, proj, proj, proj, x, mod,
      conv_w, conv_b, gmlp_norm_g, spatial_w, sbias, w_out)


def _mlp_kernel(x_ref, g_ref, sc_ref, sh_ref, gf_ref, w1_ref, w2_ref, fg_ref, o_ref,
                h_ref, acc_ref, *, final_norm):
    j = pl.program_id(1)

    @pl.when(j == 0)
    def _():
        h_ref[...] = _modulated_norm(x_ref[...], g_ref[...], sc_ref[...], sh_ref[...]).astype(BF16)
        acc_ref[...] = jnp.zeros_like(acc_ref)

    t = jnp.dot(h_ref[...], w1_ref[...], preferred_element_type=F32)
    t = jnp.square(jnp.maximum(t, 0.0)).astype(BF16)
    acc_ref[...] += jnp.dot(t, w2_ref[...], preferred_element_type=F32)

    @pl.when(j == pl.num_programs(1) - 1)
    def _():
        y = x_ref[...] + gf_ref[...] * acc_ref[...]
        if final_norm:
            y = y * lax.rsqrt(jnp.mean(y * y, axis=-1, keepdims=True) + EPS) * fg_ref[...]
        o_ref[...] = y


def _mlp(x, norm_g, mod, w1, w2, final_g, layer, seq, final_norm):
    n, d = x.shape
    d_ff = w1.shape[-1]
    bsz = n // seq
    tm = 1024
    tf = 512
    per_seq = seq // tm

    def mod_idx(k):
        return lambda i, j: ((layer * bsz + i // per_seq) * N_MOD + k, 0, 0)

    return pl.pallas_call(
        functools.partial(_mlp_kernel, final_norm=final_norm),
        grid=(n // tm, d_ff // tf),
        in_specs=[
            pl.BlockSpec((tm, d), lambda i, j: (i, 0)),
            pl.BlockSpec((None, 1, d), lambda i, j: (layer, 0, 0)),
            pl.BlockSpec((None, 1, d), mod_idx(4)),
            pl.BlockSpec((None, 1, d), mod_idx(3)),
            pl.BlockSpec((None, 1, d), mod_idx(5)),
            pl.BlockSpec((None, d, tf), lambda i, j: (layer, 0, j)),
            pl.BlockSpec((None, tf, d), lambda i, j: (layer, j, 0)),
            pl.BlockSpec((1, d), lambda i, j: (0, 0)),
        ],
        out_specs=pl.BlockSpec((tm, d), lambda i, j: (i, 0)),
        out_shape=jax.ShapeDtypeStruct((n, d), F32),
        scratch_shapes=[pltpu.VMEM((tm, d), BF16), pltpu.VMEM((tm, d), F32)],
        compiler_params=_params(("parallel", "arbitrary")),
        name="mlp",
    )(x, norm_g, mod, mod, mod, w1, w2, final_g)


def kernel(x, c, ada_w, ada_b, norm_mix_g, norm_mlp_g, w_in, conv_w, conv_b, gmlp_norm_g,
           spatial_w, spatial_b, w_out, mlp_w1, mlp_w2, final_norm_g):
    bsz, seq, d = x.shape
    depth = ada_w.shape[0]
    xf = x.reshape(bsz * seq, d)

    mod = _adaln(c, ada_w, ada_b).reshape(depth * bsz * N_MOD, 1, d)
    w_in_b = w_in.astype(BF16)
    w_out_b = w_out.astype(BF16)
    w1_b = mlp_w1.astype(BF16)
    w2_b = mlp_w2.astype(BF16)
    norm_mix = norm_mix_g.reshape(depth, 1, d)
    norm_mlp = norm_mlp_g.reshape(depth, 1, d)
    conv_b3 = conv_b.reshape(depth, 1, SC_WIDTH)
    gmlp_g3 = gmlp_norm_g.reshape(depth, 1, SG_WIDTH)
    sbias = jnp.repeat(jnp.swapaxes(spatial_b, 1, 2), HEAD_DIM, axis=2)
    final_g = final_norm_g.reshape(1, d)

    for layer in range(depth):
        proj = _norm_proj(xf, norm_mix, mod, w_in_b, layer, seq)
        a_out = _attention(proj, bsz, seq)
        xf = _mix_out(a_out, proj, xf, mod, conv_w, conv_b3, gmlp_g3, spatial_w, sbias,
                      w_out_b, layer, seq)
        xf = _mlp(xf, norm_mlp, mod, w1_b, w2_b, final_g, layer, seq,
                  final_norm=(layer == depth - 1))
    return xf.reshape(bsz, seq, d)
```

```python
import functools
import math

import jax
import jax.numpy as jnp
from jax import lax
from jax.experimental import pallas as pl
from jax.experimental.pallas import tpu as pltpu

F32 = jnp.float32
BF16 = jnp.bfloat16

HEAD_DIM = 64
SB_WIDTH = 512
SC_WIDTH = 256
SG_WIDTH = 256
SG_HEADS = 4
SG_CHUNK = 128
CHUNK = 64
N_MOD = 6
EPS = 1e-6
LANES = 128
BF16_SUBLANES = 16
LOG2E = 1.4426950408889634

VMEM_LIMIT = 48 * 1024 * 1024


def _params(sem):
    return pltpu.CompilerParams(dimension_semantics=sem, vmem_limit_bytes=VMEM_LIMIT)


def _adaln_kernel(c_ref, w_ref, b_ref, o_ref):
    c = c_ref[...]
    c_act = (c * jax.nn.sigmoid(c)).astype(BF16)
    o_ref[...] = jnp.dot(c_act, w_ref[...].astype(BF16), preferred_element_type=F32) + b_ref[...]


def _adaln(c, ada_w, ada_b):
    depth, d, width = ada_w.shape
    bsz = c.shape[0]
    tn = 1536
    return pl.pallas_call(
        _adaln_kernel,
        grid=(depth, width // tn),
        in_specs=[
            pl.BlockSpec((bsz, d), lambda l, j: (0, 0)),
            pl.BlockSpec((None, d, tn), lambda l, j: (l, 0, j)),
            pl.BlockSpec((None, 1, tn), lambda l, j: (l, 0, j)),
        ],
        out_specs=pl.BlockSpec((None, bsz, tn), lambda l, j: (l, 0, j)),
        out_shape=jax.ShapeDtypeStruct((depth, bsz, width), F32),
        compiler_params=_params(("parallel", "parallel")),
        name="adaln",
    )(c, ada_w, ada_b.reshape(depth, 1, width))


def _modulated_norm(x, g, sc, sh):
    r = lax.rsqrt(jnp.mean(x * x, axis=-1, keepdims=True) + EPS)
    return x * r * (g * (1.0 + sc)) + sh


def _norm_proj_kernel(x_ref, g_ref, sc_ref, sh_ref, w_ref, o_ref, *, col_chunks):
    h = _modulated_norm(x_ref[...], g_ref[...], sc_ref[...], sh_ref[...]).astype(BF16)
    for lo, hi in col_chunks:
        o_ref[:, lo:hi] = jnp.dot(h, w_ref[:, lo:hi], preferred_element_type=F32).astype(BF16)


def _norm_proj(x, norm_g, mod, w_in, layer, seq):
    n, d = x.shape
    width = w_in.shape[-1]
    bsz = n // seq
    tm = 512
    per_seq = seq // tm
    tn = 512
    col_chunks = tuple((lo, min(lo + tn, width)) for lo in range(0, width, tn))

    def mod_idx(k):
        return lambda i: ((layer * bsz + i // per_seq) * N_MOD + k, 0, 0)

    return pl.pallas_call(
        functools.partial(_norm_proj_kernel, col_chunks=col_chunks),
        grid=(n // tm,),
        in_specs=[
            pl.BlockSpec((tm, d), lambda i: (i, 0)),
            pl.BlockSpec((None, 1, d), lambda i: (layer, 0, 0)),
            pl.BlockSpec((None, 1, d), mod_idx(1)),
            pl.BlockSpec((None, 1, d), mod_idx(0)),
            pl.BlockSpec((None, d, width), lambda i: (layer, 0, 0)),
        ],
        out_specs=pl.BlockSpec((tm, width), lambda i: (i, 0)),
        out_shape=jax.ShapeDtypeStruct((n, width), BF16),
        compiler_params=_params(("parallel",)),
        name="norm_proj",
    )(x, norm_g, mod, mod, w_in)


def _attn_kernel(q_ref, k_ref, v_ref, o_ref, qq_ref, carry_ref, acc_ref, nz_ref, w_ref, *,
                 blk, pairs):
    qi = pl.program_id(1)
    win = 2 * blk
    rows = 2 * blk
    head0 = lax.broadcasted_iota(jnp.int32, (blk, LANES), 1) < HEAD_DIM
    qscale = -(HEAD_DIM ** -0.5) * LOG2E
    for p in range(pairs):
        q = q_ref[:, p * LANES:(p + 1) * LANES].astype(F32) * qscale
        qq_ref[p] = jnp.concatenate([jnp.where(head0, q, 0.0),
                                     jnp.where(head0, 0.0, q)], axis=0).astype(BF16)

    r = lax.broadcasted_iota(jnp.int32, (win, win), 0)
    c = lax.broadcasted_iota(jnp.int32, (win, win), 1)
    cum = jnp.where(r >= c, 1.0, 0.0).astype(BF16)

    first = qi // 2
    t_row = lax.broadcasted_iota(jnp.int32, (rows, win), 0) & (blk - 1)
    s_col = lax.broadcasted_iota(jnp.int32, (rows, win), 1)
    causal = s_col + (first * win - qi * blk) < t_row

    ps = range(pairs)

    def key_start(w):
        return pl.multiple_of(jnp.maximum(w, 0) * win, win)

    def scores(w, slot):
        start = key_start(w)
        for p in ps:
            nz_ref[slot, p] = lax.dot_general(
                qq_ref[p], k_ref[pl.ds(start, win), p * LANES:(p + 1) * LANES],
                (((1,), (1,)), ((), ())), preferred_element_type=F32)

    def weights_times_v(w, slot):
        start = key_start(w)
        for p in ps:
            acc_ref[p] += jnp.dot(w_ref[slot, p],
                                  v_ref[pl.ds(start, win), p * LANES:(p + 1) * LANES],
                                  preferred_element_type=F32)

    def weights(slot, diagonal):
        hl = []
        for p in ps:
            nz = nz_ref[slot, p]
            sp = jnp.log(1.0 + jnp.exp2(-jnp.abs(nz))) * LOG2E
            log_stay = jnp.minimum(nz, 0.0) - sp
            if diagonal:
                log_stay = jnp.where(causal, log_stay, 0.0)
            hl.append(log_stay.astype(BF16))
        incl = [jnp.dot(hl[p], cum, preferred_element_type=F32) for p in ps]
        for p in ps:
            if diagonal:
                w = jnp.where(causal, jnp.exp2(incl[p] - nz_ref[slot, p]), 0.0)
                carry_ref[p] = incl[p][:, 0:1]
            else:
                carry = carry_ref[p]
                w = jnp.exp2(incl[p] + carry - nz_ref[slot, p])
                carry_ref[p] = carry + incl[p][:, 0:1]
            w_ref[slot, p] = w.astype(BF16)

    def step(w, slot):
        scores(w - 1, 1 - slot)
        weights_times_v(w + 1, 1 - slot)
        weights(slot, False)

    acc_ref[...] = jnp.zeros_like(acc_ref)
    scores(first, first & 1)
    scores(first - 1, (first - 1) & 1)
    weights(first & 1, True)

    @pl.when(first & 1 == 1)
    def _():
        step(first - 1, 0)

    def body(m, _):
        w = first - (first & 1) - 1 - 2 * m
        step(w, 1)
        step(w - 1, 0)
        return 0

    lax.fori_loop(0, first // 2, body, 0)
    weights_times_v(0, 0)
    for p in range(pairs):
        acc = acc_ref[p]
        o_ref[:, p * LANES:(p + 1) * LANES] = jnp.where(head0, acc[:blk], acc[blk:]).astype(BF16)


def _attention(proj, bsz, seq):
    width = proj.shape[-1]
    proj3 = proj.reshape(bsz, seq, width)
    blk = 128
    pairs = SB_WIDTH // LANES
    return pl.pallas_call(
        functools.partial(_attn_kernel, blk=blk, pairs=pairs),
        grid=(bsz, seq // blk),
        in_specs=[
            pl.BlockSpec((None, blk, SB_WIDTH), lambda b, i: (b, i, 0)),
            pl.BlockSpec((None, seq, SB_WIDTH), lambda b, i: (b, 0, 1)),
            pl.BlockSpec((None, seq, SB_WIDTH), lambda b, i: (b, 0, 2)),
        ],
        out_specs=pl.BlockSpec((None, blk, SB_WIDTH), lambda b, i: (b, i, 0)),
        out_shape=jax.ShapeDtypeStruct((bsz, seq, SB_WIDTH), BF16),
        scratch_shapes=[pltpu.VMEM((pairs, 2 * blk, LANES), BF16),
                        pltpu.VMEM((pairs, 2 * blk, 1), F32),
                        pltpu.VMEM((pairs, 2 * blk, LANES), F32),
                        pltpu.VMEM((2, pairs, 2 * blk, 2 * blk), F32),
                        pltpu.VMEM((2, pairs, 2 * blk, 2 * blk), BF16)],
        compiler_params=_params(("parallel", "arbitrary")),
        name="sb_attention",
    )(proj3, proj3, proj3).reshape(bsz * seq, SB_WIDTH)


def _gelu(x):
    return 0.5 * x * (1.0 + jnp.tanh(0.7978845608028654 * (x + 0.044715 * (x * x * x))))


def _mix_out_kernel(a_ref, bg_ref, cg_ref, hc_ref, us_ref, vs_ref, cgh_ref, hch_ref,
                    x_ref, gm_ref, cw_ref, cb_ref, ng_ref, sw_ref, sbias_ref, wo_ref,
                    o_ref, uext_ref, *, tm, per_seq, halo):
    i = pl.program_id(0)

    u = cg_ref[...].astype(F32) * hc_ref[...].astype(F32)
    u_halo = cgh_ref[...].astype(F32) * hch_ref[...].astype(F32)
    u_halo = jnp.where(i % per_seq == 0, 0.0, u_halo)
    uext_ref[0:halo, :] = u_halo
    uext_ref[halo:halo + tm, :] = u
    u_m1 = uext_ref[halo - 1:halo - 1 + tm, :]
    u_m2 = uext_ref[halo - 2:halo - 2 + tm, :]
    cw = cw_ref[...]
    y = cw[0:1] * u_m2 + cw[1:2] * u_m1 + cw[2:3] * u + cb_ref[...]
    c_out = bg_ref[...].astype(F32) * y

    ug = _gelu(us_ref[...].astype(F32))
    vg = _gelu(vs_ref[...].astype(F32))
    vn = vg * lax.rsqrt(jnp.mean(vg * vg, axis=-1, keepdims=True) + EPS) * ng_ref[...]
    t_chunk = lax.broadcasted_iota(jnp.int32, (SG_CHUNK, SG_CHUNK), 0) // CHUNK
    s_chunk = lax.broadcasted_iota(jnp.int32, (SG_CHUNK, SG_CHUNK), 1) // CHUNK
    chunk_causal = t_chunk >= s_chunk
    w_cat = jnp.concatenate(
        [jnp.where(chunk_causal, sw_ref[h], 0.0) for h in range(SG_HEADS)], axis=1).astype(BF16)
    lane_head = lax.broadcasted_iota(jnp.int32, (SG_CHUNK, SG_WIDTH), 1) // HEAD_DIM
    sbias = sbias_ref[...]
    mixed = []
    for n in range(tm // SG_CHUNK):
        v_win = vn[n * SG_CHUNK:(n + 1) * SG_CHUNK]
        v_stack = jnp.concatenate(
            [jnp.where(lane_head == h, v_win, 0.0) for h in range(SG_HEADS)], axis=0).astype(BF16)
        mixed.append(jnp.dot(w_cat, v_stack, preferred_element_type=F32) + sbias)
    s_out = ug * jnp.concatenate(mixed, axis=0)

    cat = jnp.concatenate([a_ref[...], c_out.astype(BF16), s_out.astype(BF16)], axis=1)
    mix = jnp.dot(cat, wo_ref[...], preferred_element_type=F32)
    o_ref[...] = x_ref[...] + gm_ref[...] * mix


def _mix_out(a_out, proj, x, mod, conv_w, conv_b, gmlp_norm_g, spatial_w, sbias, w_out, layer, seq):
    n, d = x.shape
    bsz = n // seq
    tm = 512
    per_seq = seq // tm
    halo = BF16_SUBLANES
    col0 = 3 * SB_WIDTH // SC_WIDTH

    def col(k):
        return pl.BlockSpec((tm, SC_WIDTH), lambda i: (i, col0 + k))

    def halo_col(k):
        return pl.BlockSpec((halo, SC_WIDTH),
                            lambda i: (jnp.maximum(i * (tm // halo) - 1, 0), col0 + k))

    return pl.pallas_call(
        functools.partial(_mix_out_kernel, tm=tm, per_seq=per_seq, halo=halo),
        grid=(n // tm,),
        in_specs=[
            pl.BlockSpec((tm, SB_WIDTH), lambda i: (i, 0)),
            col(0), col(1), col(2), col(3), col(4),
            halo_col(1), halo_col(2),
            pl.BlockSpec((tm, d), lambda i: (i, 0)),
            pl.BlockSpec((None, 1, d), lambda i: ((layer * bsz + i // per_seq) * N_MOD + 2, 0, 0)),
            pl.BlockSpec((None, 3, SC_WIDTH), lambda i: (layer, 0, 0)),
            pl.BlockSpec((None, 1, SC_WIDTH), lambda i: (layer, 0, 0)),
            pl.BlockSpec((None, 1, SG_WIDTH), lambda i: (layer, 0, 0)),
            pl.BlockSpec((None, SG_HEADS, SG_CHUNK, SG_CHUNK), lambda i: (layer, 0, 0, 0)),
            pl.BlockSpec((None, SG_CHUNK, SG_WIDTH), lambda i: (layer, 0, 0)),
            pl.BlockSpec((None, d, d), lambda i: (layer, 0, 0)),
        ],
        out_specs=pl.BlockSpec((tm, d), lambda i: (i, 0)),
        out_shape=jax.ShapeDtypeStruct((n, d), F32),
        scratch_shapes=[pltpu.VMEM((halo + tm, SC_WIDTH), F32)],
        compiler_params=_params(("parallel",)),
        name="mix_out",
    )(a_out, proj, proj, proj, proj, proj, proj, proj, x, mod,
      conv_w, conv_b, gmlp_norm_g, spatial_w, sbias, w_out)


def _mlp_kernel(x_ref, g_ref, sc_ref, sh_ref, gf_ref, w1_ref, w2_ref, fg_ref, o_ref,
                h_ref, acc_ref, *, final_norm):
    j = pl.program_id(1)

    @pl.when(j == 0)
    def _():
        h_ref[...] = _modulated_norm(x_ref[...], g_ref[...], sc_ref[...], sh_ref[...]).astype(BF16)
        acc_ref[...] = jnp.zeros_like(acc_ref)

    t = jnp.dot(h_ref[...], w1_ref[...], preferred_element_type=F32)
    t = jnp.square(jnp.maximum(t, 0.0)).astype(BF16)
    acc_ref[...] += jnp.dot(t, w2_ref[...], preferred_element_type=F32)

    @pl.when(j == pl.num_programs(1) - 1)
    def _():
        y = x_ref[...] + gf_ref[...] * acc_ref[...]
        if final_norm:
            y = y * lax.rsqrt(jnp.mean(y * y, axis=-1, keepdims=True) + EPS) * fg_ref[...]
        o_ref[...] = y


def _mlp(x, norm_g, mod, w1, w2, final_g, layer, seq, final_norm):
    n, d = x.shape
    d_ff = w1.shape[-1]
    bsz = n // seq
    tm = 1024
    tf = 512
    per_seq = seq // tm

    def mod_idx(k):
        return lambda i, j: ((layer * bsz + i // per_seq) * N_MOD + k, 0, 0)

    return pl.pallas_call(
        functools.partial(_mlp_kernel, final_norm=final_norm),
        grid=(n // tm, d_ff // tf),
        in_specs=[
            pl.BlockSpec((tm, d), lambda i, j: (i, 0)),
            pl.BlockSpec((None, 1, d), lambda i, j: (layer, 0, 0)),
            pl.BlockSpec((None, 1, d), mod_idx(4)),
            pl.BlockSpec((None, 1, d), mod_idx(3)),
            pl.BlockSpec((None, 1, d), mod_idx(5)),
            pl.BlockSpec((None, d, tf), lambda i, j: (layer, 0, j)),
            pl.BlockSpec((None, tf, d), lambda i, j: (layer, j, 0)),
            pl.BlockSpec((1, d), lambda i, j: (0, 0)),
        ],
        out_specs=pl.BlockSpec((tm, d), lambda i, j: (i, 0)),
        out_shape=jax.ShapeDtypeStruct((n, d), F32),
        scratch_shapes=[pltpu.VMEM((tm, d), BF16), pltpu.VMEM((tm, d), F32)],
        compiler_params=_params(("parallel", "arbitrary")),
        name="mlp",
    )(x, norm_g, mod, mod, mod, w1, w2, final_g)


def kernel(x, c, ada_w, ada_b, norm_mix_g, norm_mlp_g, w_in, conv_w, conv_b, gmlp_norm_g,
           spatial_w, spatial_b, w_out, mlp_w1, mlp_w2, final_norm_g):
    bsz, seq, d = x.shape
    depth = ada_w.shape[0]
    xf = x.reshape(bsz * seq, d)

    mod = _adaln(c, ada_w, ada_b).reshape(depth * bsz * N_MOD, 1, d)
    w_in_b = w_in.astype(BF16)
    w_out_b = w_out.astype(BF16)
    w1_b = mlp_w1.astype(BF16)
    w2_b = mlp_w2.astype(BF16)
    norm_mix = norm_mix_g.reshape(depth, 1, d)
    norm_mlp = norm_mlp_g.reshape(depth, 1, d)
    conv_b3 = conv_b.reshape(depth, 1, SC_WIDTH)
    gmlp_g3 = gmlp_norm_g.reshape(depth, 1, SG_WIDTH)
    sbias = jnp.repeat(jnp.swapaxes(spatial_b, 1, 2), HEAD_DIM, axis=2)
    final_g = final_norm_g.reshape(1, d)

    for layer in range(depth):
        proj = _norm_proj(xf, norm_mix, mod, w_in_b, layer, seq)
        a_out = _attention(proj, bsz, seq)
        xf = _mix_out(a_out, proj, xf, mod, conv_w, conv_b3, gmlp_g3, spatial_w, sbias,
                      w_out_b, layer, seq)
        xf = _mlp(xf, norm_mlp, mod, w1_b, w2_b, final_g, layer, seq,
                  final_norm=(layer == depth - 1))
    return xf.reshape(bsz, seq, d)
```

```python
import functools
import math

import jax
import jax.numpy as jnp
from jax import lax
from jax.experimental import pallas as pl
from jax.experimental.pallas import tpu as pltpu

F32 = jnp.float32
BF16 = jnp.bfloat16

HEAD_DIM = 64
SB_WIDTH = 512
SC_WIDTH = 256
SG_WIDTH = 256
SG_HEADS = 4
SG_CHUNK = 128
CHUNK = 64
N_MOD = 6
EPS = 1e-6
LANES = 128
BF16_SUBLANES = 16
LOG2E = 1.4426950408889634

VMEM_LIMIT = 48 * 1024 * 1024


def _params(sem):
    return pltpu.CompilerParams(dimension_semantics=sem, vmem_limit_bytes=VMEM_LIMIT)


def _adaln_kernel(c_ref, w_ref, b_ref, o_ref):
    c = c_ref[...]
    c_act = (c * jax.nn.sigmoid(c)).astype(BF16)
    o_ref[...] = jnp.dot(c_act, w_ref[...].astype(BF16), preferred_element_type=F32) + b_ref[...]


def _adaln(c, ada_w, ada_b):
    depth, d, width = ada_w.shape
    bsz = c.shape[0]
    tn = 1536
    return pl.pallas_call(
        _adaln_kernel,
        grid=(depth, width // tn),
        in_specs=[
            pl.BlockSpec((bsz, d), lambda l, j: (0, 0)),
            pl.BlockSpec((None, d, tn), lambda l, j: (l, 0, j)),
            pl.BlockSpec((None, 1, tn), lambda l, j: (l, 0, j)),
        ],
        out_specs=pl.BlockSpec((None, bsz, tn), lambda l, j: (l, 0, j)),
        out_shape=jax.ShapeDtypeStruct((depth, bsz, width), F32),
        compiler_params=_params(("parallel", "parallel")),
        name="adaln",
    )(c, ada_w, ada_b.reshape(depth, 1, width))


def _modulated_norm(x, g, sc, sh):
    r = lax.rsqrt(jnp.mean(x * x, axis=-1, keepdims=True) + EPS)
    return x * r * (g * (1.0 + sc)) + sh


def _norm_proj_kernel(x_ref, g_ref, sc_ref, sh_ref, w_ref, o_ref, *, col_chunks):
    h = _modulated_norm(x_ref[...], g_ref[...], sc_ref[...], sh_ref[...]).astype(BF16)
    for lo, hi in col_chunks:
        o_ref[:, lo:hi] = jnp.dot(h, w_ref[:, lo:hi], preferred_element_type=F32).astype(BF16)


def _norm_proj(x, norm_g, mod, w_in, layer, seq):
    n, d = x.shape
    width = w_in.shape[-1]
    bsz = n // seq
    tm = 512
    per_seq = seq // tm
    tn = 512
    col_chunks = tuple((lo, min(lo + tn, width)) for lo in range(0, width, tn))

    def mod_idx(k):
        return lambda i: ((layer * bsz + i // per_seq) * N_MOD + k, 0, 0)

    return pl.pallas_call(
        functools.partial(_norm_proj_kernel, col_chunks=col_chunks),
        grid=(n // tm,),
        in_specs=[
            pl.BlockSpec((tm, d), lambda i: (i, 0)),
            pl.BlockSpec((None, 1, d), lambda i: (layer, 0, 0)),
            pl.BlockSpec((None, 1, d), mod_idx(1)),
            pl.BlockSpec((None, 1, d), mod_idx(0)),
            pl.BlockSpec((None, d, width), lambda i: (layer, 0, 0)),
        ],
        out_specs=pl.BlockSpec((tm, width), lambda i: (i, 0)),
        out_shape=jax.ShapeDtypeStruct((n, width), BF16),
        compiler_params=_params(("parallel",)),
        name="norm_proj",
    )(x, norm_g, mod, mod, w_in)


ZERO_WEIGHT_LOG2 = -140.0


def _attn_kernel(q_ref, k_ref, v_ref, o_ref, qq_ref, carry_ref, acc_ref, nz_ref, w_ref,
                 kmax_ref, *, blk, pairs):
    qi = pl.program_id(1)
    win = blk
    rows = 2 * blk
    head0 = lax.broadcasted_iota(jnp.int32, (blk, LANES), 1) < HEAD_DIM
    qscale = -(HEAD_DIM ** -0.5) * LOG2E

    @pl.when(qi == 0)
    def _():
        kmax_ref[0] = jnp.max(jnp.abs(k_ref[...]).astype(F32))

    q_l1 = jnp.zeros((), F32)
    for p in range(pairs):
        q = q_ref[:, p * LANES:(p + 1) * LANES].astype(F32) * qscale
        qq = jnp.concatenate([jnp.where(head0, q, 0.0),
                              jnp.where(head0, 0.0, q)], axis=0).astype(BF16)
        qq_ref[p] = qq
        q_l1 = jnp.maximum(q_l1, jnp.max(jnp.sum(jnp.abs(qq.astype(F32)), axis=1)))
    stop_below = ZERO_WEIGHT_LOG2 - 2.0 ** -8 * (1.01 * q_l1 * kmax_ref[0] + 2.0)

    r = lax.broadcasted_iota(jnp.int32, (win, win), 0)
    c = lax.broadcasted_iota(jnp.int32, (win, win), 1)
    cum = jnp.where(r >= c, 1.0, 0.0).astype(BF16)

    first = qi
    t_row = lax.broadcasted_iota(jnp.int32, (rows, win), 0) & (blk - 1)
    s_col = lax.broadcasted_iota(jnp.int32, (rows, win), 1)
    causal = s_col < t_row

    ps = range(pairs)

    def key_start(w):
        return pl.multiple_of(jnp.maximum(w, 0) * win, win)

    def scores(w, slot):
        start = key_start(w)
        for p in ps:
            nz_ref[slot, p] = lax.dot_general(
                qq_ref[p], k_ref[pl.ds(start, win), p * LANES:(p + 1) * LANES],
                (((1,), (1,)), ((), ())), preferred_element_type=F32)

    def weights_times_v(w, slot):
        start = key_start(w)
        for p in ps:
            acc_ref[p] += jnp.dot(w_ref[slot, p],
                                  v_ref[pl.ds(start, win), p * LANES:(p + 1) * LANES],
                                  preferred_element_type=F32)

    def weights(slot, diagonal):
        hl = []
        for p in ps:
            nz = nz_ref[slot, p]
            sp = jnp.log(1.0 + jnp.exp2(-jnp.abs(nz))) * LOG2E
            log_stay = jnp.minimum(nz, 0.0) - sp
            if diagonal:
                log_stay = jnp.where(causal, log_stay, 0.0)
            hl.append(log_stay.astype(BF16))
        incl = [jnp.dot(hl[p], cum, preferred_element_type=F32) for p in ps]
        for p in ps:
            if diagonal:
                w = jnp.where(causal, jnp.exp2(incl[p] - nz_ref[slot, p]), 0.0)
                carry_ref[p] = incl[p][:, 0:1]
            else:
                carry = carry_ref[p]
                w = jnp.exp2(incl[p] + carry - nz_ref[slot, p])
                carry_ref[p] = carry + incl[p][:, 0:1]
            w_ref[slot, p] = w.astype(BF16)

    def step(w, slot):
        scores(w - 1, 1 - slot)
        weights_times_v(w + 1, 1 - slot)
        weights(slot, False)

    def unfinished():
        return jnp.max(carry_ref[...]) > stop_below

    acc_ref[...] = jnp.zeros_like(acc_ref)
    scores(first, first & 1)
    scores(first - 1, (first - 1) & 1)
    weights(first & 1, True)

    parity_step = (first & 1 == 1) & unfinished()

    @pl.when(parity_step)
    def _():
        step(first - 1, 0)

    w0 = first - (first & 1) - 1
    go0 = (w0 >= 1) & ((first & 1 == 0) | parity_step) & unfinished()
    last0 = jnp.where(parity_step, first - 1, first)

    def body(state):
        w, _, _ = state
        step(w, 1)
        go_even = unfinished()

        @pl.when(go_even)
        def _():
            step(w - 1, 0)

        go_on = go_even & (w - 2 >= 1) & unfinished()
        return w - 2, go_on, jnp.where(go_even, w - 1, w)

    _, _, last = lax.while_loop(lambda state: state[1], body, (w0, go0, last0))
    weights_times_v(last, last & 1)
    for p in range(pairs):
        acc = acc_ref[p]
        o_ref[:, p * LANES:(p + 1) * LANES] = jnp.where(head0, acc[:blk], acc[blk:]).astype(BF16)


def _attention(proj, bsz, seq):
    width = proj.shape[-1]
    proj3 = proj.reshape(bsz, seq, width)
    blk = 256
    pairs = SB_WIDTH // LANES
    return pl.pallas_call(
        functools.partial(_attn_kernel, blk=blk, pairs=pairs),
        grid=(bsz, seq // blk),
        in_specs=[
            pl.BlockSpec((None, blk, SB_WIDTH), lambda b, i: (b, i, 0)),
            pl.BlockSpec((None, seq, SB_WIDTH), lambda b, i: (b, 0, 1)),
            pl.BlockSpec((None, seq, SB_WIDTH), lambda b, i: (b, 0, 2)),
        ],
        out_specs=pl.BlockSpec((None, blk, SB_WIDTH), lambda b, i: (b, i, 0)),
        out_shape=jax.ShapeDtypeStruct((bsz, seq, SB_WIDTH), BF16),
        scratch_shapes=[pltpu.VMEM((pairs, 2 * blk, LANES), BF16),
                        pltpu.VMEM((pairs, 2 * blk, 1), F32),
                        pltpu.VMEM((pairs, 2 * blk, LANES), F32),
                        pltpu.VMEM((2, pairs, 2 * blk, blk), F32),
                        pltpu.VMEM((2, pairs, 2 * blk, blk), BF16),
                        pltpu.SMEM((1,), F32)],
        compiler_params=_params(("parallel", "arbitrary")),
        name="sb_attention",
    )(proj3, proj3, proj3).reshape(bsz * seq, SB_WIDTH)


def _gelu(x):
    return 0.5 * x * (1.0 + jnp.tanh(0.7978845608028654 * (x + 0.044715 * (x * x * x))))


def _mix_out_kernel(a_ref, bg_ref, cg_ref, hc_ref, us_ref, vs_ref, cgh_ref, hch_ref,
                    x_ref, gm_ref, cw_ref, cb_ref, ng_ref, sw_ref, sbias_ref, wo_ref,
                    o_ref, uext_ref, *, tm, per_seq, halo):
    i = pl.program_id(0)

    u = cg_ref[...].astype(F32) * hc_ref[...].astype(F32)
    u_halo = cgh_ref[...].astype(F32) * hch_ref[...].astype(F32)
    u_halo = jnp.where(i % per_seq == 0, 0.0, u_halo)
    uext_ref[0:halo, :] = u_halo
    uext_ref[halo:halo + tm, :] = u
    u_m1 = uext_ref[halo - 1:halo - 1 + tm, :]
    u_m2 = uext_ref[halo - 2:halo - 2 + tm, :]
    cw = cw_ref[...]
    y = cw[0:1] * u_m2 + cw[1:2] * u_m1 + cw[2:3] * u + cb_ref[...]
    c_out = bg_ref[...].astype(F32) * y

    ug = _gelu(us_ref[...].astype(F32))
    vg = _gelu(vs_ref[...].astype(F32))
    vn = vg * lax.rsqrt(jnp.mean(vg * vg, axis=-1, keepdims=True) + EPS) * ng_ref[...]
    t_chunk = lax.broadcasted_iota(jnp.int32, (SG_CHUNK, SG_CHUNK), 0) // CHUNK
    s_chunk = lax.broadcasted_iota(jnp.int32, (SG_CHUNK, SG_CHUNK), 1) // CHUNK
    chunk_causal = t_chunk >= s_chunk
    w_cat = jnp.concatenate(
        [jnp.where(chunk_causal, sw_ref[h], 0.0) for h in range(SG_HEADS)], axis=1).astype(BF16)
    lane_head = lax.broadcasted_iota(jnp.int32, (SG_CHUNK, SG_WIDTH), 1) // HEAD_DIM
    sbias = sbias_ref[...]
    mixed = []
    for n in range(tm // SG_CHUNK):
        v_win = vn[n * SG_CHUNK:(n + 1) * SG_CHUNK]
        v_stack = jnp.concatenate(
            [jnp.where(lane_head == h, v_win, 0.0) for h in range(SG_HEADS)], axis=0).astype(BF16)
        mixed.append(jnp.dot(w_cat, v_stack, preferred_element_type=F32) + sbias)
    s_out = ug * jnp.concatenate(mixed, axis=0)

    cat = jnp.concatenate([a_ref[...], c_out.astype(BF16), s_out.astype(BF16)], axis=1)
    mix = jnp.dot(cat, wo_ref[...], preferred_element_type=F32)
    o_ref[...] = x_ref[...] + gm_ref[...] * mix


def _mix_out(a_out, proj, x, mod, conv_w, conv_b, gmlp_norm_g, spatial_w, sbias, w_out, layer, seq):
    n, d = x.shape
    bsz = n // seq
    tm = 512
    per_seq = seq // tm
    halo = BF16_SUBLANES
    col0 = 3 * SB_WIDTH // SC_WIDTH

    def col(k):
        return pl.BlockSpec((tm, SC_WIDTH), lambda i: (i, col0 + k))

    def halo_col(k):
        return pl.BlockSpec((halo, SC_WIDTH),
                            lambda i: (jnp.maximum(i * (tm // halo) - 1, 0), col0 + k))

    return pl.pallas_call(
        functools.partial(_mix_out_kernel, tm=tm, per_seq=per_seq, halo=halo),
        grid=(n // tm,),
        in_specs=[
            pl.BlockSpec((tm, SB_WIDTH), lambda i: (i, 0)),
            col(0), col(1), col(2), col(3), col(4),
            halo_col(1), halo_col(2),
            pl.BlockSpec((tm, d), lambda i: (i, 0)),
            pl.BlockSpec((None, 1, d), lambda i: ((layer * bsz + i // per_seq) * N_MOD + 2, 0, 0)),
            pl.BlockSpec((None, 3, SC_WIDTH), lambda i: (layer, 0, 0)),
            pl.BlockSpec((None, 1, SC_WIDTH), lambda i: (layer, 0, 0)),
            pl.BlockSpec((None, 1, SG_WIDTH), lambda i: (layer, 0, 0)),
            pl.BlockSpec((None, SG_HEADS, SG_CHUNK, SG_CHUNK), lambda i: (layer, 0, 0, 0)),
            pl.BlockSpec((None, SG_CHUNK, SG_WIDTH), lambda i: (layer, 0, 0)),
            pl.BlockSpec((None, d, d), lambda i: (layer, 0, 0)),
        ],
        out_specs=pl.BlockSpec((tm, d), lambda i: (i, 0)),
        out_shape=jax.ShapeDtypeStruct((n, d), F32),
        scratch_shapes=[pltpu.VMEM((halo + tm, SC_WIDTH), F32)],
        compiler_params=_params(("parallel",)),
        name="mix_out",
    )(a_out, proj, proj, proj, proj, proj, proj, proj, x, mod,
      conv_w, conv_b, gmlp_norm_g, spatial_w, sbias, w_out)


def _mlp_kernel(x_ref, g_ref, sc_ref, sh_ref, gf_ref, w1_ref, w2_ref, fg_ref, o_ref,
                h_ref, acc_ref, *, final_norm):
    j = pl.program_id(1)

    @pl.when(j == 0)
    def _():
        h_ref[...] = _modulated_norm(x_ref[...], g_ref[...], sc_ref[...], sh_ref[...]).astype(BF16)
        acc_ref[...] = jnp.zeros_like(acc_ref)

    t = jnp.dot(h_ref[...], w1_ref[...], preferred_element_type=F32)
    t = jnp.square(jnp.maximum(t, 0.0)).astype(BF16)
    acc_ref[...] += jnp.dot(t, w2_ref[...], preferred_element_type=F32)

    @pl.when(j == pl.num_programs(1) - 1)
    def _():
        y = x_ref[...] + gf_ref[...] * acc_ref[...]
        if final_norm:
            y = y * lax.rsqrt(jnp.mean(y * y, axis=-1, keepdims=True) + EPS) * fg_ref[...]
        o_ref[...] = y


def _mlp(x, norm_g, mod, w1, w2, final_g, layer, seq, final_norm):
    n, d = x.shape
    d_ff = w1.shape[-1]
    bsz = n // seq
    tm = 1024
    tf = 512
    per_seq = seq // tm

    def mod_idx(k):
        return lambda i, j: ((layer * bsz + i // per_seq) * N_MOD + k, 0, 0)

    return pl.pallas_call(
        functools.partial(_mlp_kernel, final_norm=final_norm),
        grid=(n // tm, d_ff // tf),
        in_specs=[
            pl.BlockSpec((tm, d), lambda i, j: (i, 0)),
            pl.BlockSpec((None, 1, d), lambda i, j: (layer, 0, 0)),
            pl.BlockSpec((None, 1, d), mod_idx(4)),
            pl.BlockSpec((None, 1, d), mod_idx(3)),
            pl.BlockSpec((None, 1, d), mod_idx(5)),
            pl.BlockSpec((None, d, tf), lambda i, j: (layer, 0, j)),
            pl.BlockSpec((None, tf, d), lambda i, j: (layer, j, 0)),
            pl.BlockSpec((1, d), lambda i, j: (0, 0)),
        ],
        out_specs=pl.BlockSpec((tm, d), lambda i, j: (i, 0)),
        out_shape=jax.ShapeDtypeStruct((n, d), F32),
        scratch_shapes=[pltpu.VMEM((tm, d), BF16), pltpu.VMEM((tm, d), F32)],
        compiler_params=_params(("parallel", "arbitrary")),
        name="mlp",
    )(x, norm_g, mod, mod, mod, w1, w2, final_g)


def kernel(x, c, ada_w, ada_b, norm_mix_g, norm_mlp_g, w_in, conv_w, conv_b, gmlp_norm_g,
           spatial_w, spatial_b, w_out, mlp_w1, mlp_w2, final_norm_g):
    bsz, seq, d = x.shape
    depth = ada_w.shape[0]
    xf = x.reshape(bsz * seq, d)

    mod = _adaln(c, ada_w, ada_b).reshape(depth * bsz * N_MOD, 1, d)
    w_in_b = w_in.astype(BF16)
    w_out_b = w_out.astype(BF16)
    w1_b = mlp_w1.astype(BF16)
    w2_b = mlp_w2.astype(BF16)
    norm_mix = norm_mix_g.reshape(depth, 1, d)
    norm_mlp = norm_mlp_g.reshape(depth, 1, d)
    conv_b3 = conv_b.reshape(depth, 1, SC_WIDTH)
    gmlp_g3 = gmlp_norm_g.reshape(depth, 1, SG_WIDTH)
    sbias = jnp.repeat(jnp.swapaxes(spatial_b, 1, 2), HEAD_DIM, axis=2)
    final_g = final_norm_g.reshape(1, d)

    for layer in range(depth):
        proj = _norm_proj(xf, norm_mix, mod, w_in_b, layer, seq)
        a_out = _attention(proj, bsz, seq)
        xf = _mix_out(a_out, proj, xf, mod, conv_w, conv_b3, gmlp_g3, spatial_w, sbias,
                      w_out_b, layer, seq)
        xf = _mlp(xf, norm_mlp, mod, w1_b, w2_b, final_g, layer, seq,
                  final_norm=(layer == depth - 1))
    return xf.reshape(bsz, seq, d)
```

```python
import functools
import math

import jax
import jax.numpy as jnp
from jax import lax
from jax.experimental import pallas as pl
from jax.experimental.pallas import tpu as pltpu

F32 = jnp.float32
BF16 = jnp.bfloat16

HEAD_DIM = 64
SB_WIDTH = 512
SC_WIDTH = 256
SG_WIDTH = 256
SG_HEADS = 4
SG_CHUNK = 128
CHUNK = 64
N_MOD = 6
EPS = 1e-6
LANES = 128
BF16_SUBLANES = 16
LOG2E = 1.4426950408889634

VMEM_LIMIT = 48 * 1024 * 1024


def _params(sem):
    return pltpu.CompilerParams(dimension_semantics=sem, vmem_limit_bytes=VMEM_LIMIT)


def _adaln_kernel(c_ref, w_ref, b_ref, o_ref):
    c = c_ref[...]
    c_act = (c * jax.nn.sigmoid(c)).astype(BF16)
    o_ref[...] = jnp.dot(c_act, w_ref[...].astype(BF16), preferred_element_type=F32) + b_ref[...]


def _adaln(c, ada_w, ada_b):
    depth, d, width = ada_w.shape
    bsz = c.shape[0]
    tn = 1536
    return pl.pallas_call(
        _adaln_kernel,
        grid=(depth, width // tn),
        in_specs=[
            pl.BlockSpec((bsz, d), lambda l, j: (0, 0)),
            pl.BlockSpec((None, d, tn), lambda l, j: (l, 0, j)),
            pl.BlockSpec((None, 1, tn), lambda l, j: (l, 0, j)),
        ],
        out_specs=pl.BlockSpec((None, bsz, tn), lambda l, j: (l, 0, j)),
        out_shape=jax.ShapeDtypeStruct((depth, bsz, width), F32),
        compiler_params=_params(("parallel", "parallel")),
        name="adaln",
    )(c, ada_w, ada_b.reshape(depth, 1, width))


def _modulated_norm(x, g, sc, sh):
    r = lax.rsqrt(jnp.mean(x * x, axis=-1, keepdims=True) + EPS)
    return x * r * (g * (1.0 + sc)) + sh


def _norm_proj_kernel(x_ref, g_ref, sc_ref, sh_ref, w_ref, o_ref, *, col_chunks):
    h = _modulated_norm(x_ref[...], g_ref[...], sc_ref[...], sh_ref[...]).astype(BF16)
    for lo, hi in col_chunks:
        o_ref[:, lo:hi] = jnp.dot(h, w_ref[:, lo:hi], preferred_element_type=F32).astype(BF16)


def _norm_proj(x, norm_g, mod, w_in, layer, seq):
    n, d = x.shape
    width = w_in.shape[-1]
    bsz = n // seq
    tm = 512
    per_seq = seq // tm
    tn = 512
    col_chunks = tuple((lo, min(lo + tn, width)) for lo in range(0, width, tn))

    def mod_idx(k):
        return lambda i: ((layer * bsz + i // per_seq) * N_MOD + k, 0, 0)

    return pl.pallas_call(
        functools.partial(_norm_proj_kernel, col_chunks=col_chunks),
        grid=(n // tm,),
        in_specs=[
            pl.BlockSpec((tm, d), lambda i: (i, 0)),
            pl.BlockSpec((None, 1, d), lambda i: (layer, 0, 0)),
            pl.BlockSpec((None, 1, d), mod_idx(1)),
            pl.BlockSpec((None, 1, d), mod_idx(0)),
            pl.BlockSpec((None, d, width), lambda i: (layer, 0, 0)),
        ],
        out_specs=pl.BlockSpec((tm, width), lambda i: (i, 0)),
        out_shape=jax.ShapeDtypeStruct((n, width), BF16),
        compiler_params=_params(("parallel",)),
        name="norm_proj",
    )(x, norm_g, mod, mod, w_in)


ZERO_WEIGHT_LOG2 = -140.0


def _attn_kernel(q_ref, k_ref, v_ref, o_ref, qq_ref, carry_ref, acc_ref, nz_ref, w_ref,
                 kmax_ref, more_ref, *, blk, pairs):
    qi = pl.program_id(1)
    win = blk
    rows = 2 * blk
    head0 = lax.broadcasted_iota(jnp.int32, (blk, LANES), 1) < HEAD_DIM
    qscale = -(HEAD_DIM ** -0.5) * LOG2E

    @pl.when(qi == 0)
    def _():
        kmax_ref[0] = jnp.max(jnp.abs(k_ref[...]).astype(F32))

    q_l1 = jnp.zeros((), F32)
    for p in range(pairs):
        q = q_ref[:, p * LANES:(p + 1) * LANES].astype(F32) * qscale
        qq = jnp.concatenate([jnp.where(head0, q, 0.0),
                              jnp.where(head0, 0.0, q)], axis=0).astype(BF16)
        qq_ref[p] = qq
        q_l1 = jnp.maximum(q_l1, jnp.max(jnp.sum(jnp.abs(qq.astype(F32)), axis=1)))
    stop_below = ZERO_WEIGHT_LOG2 - 2.0 ** -8 * (1.01 * q_l1 * kmax_ref[0] + 2.0)

    r = lax.broadcasted_iota(jnp.int32, (win, win), 0)
    c = lax.broadcasted_iota(jnp.int32, (win, win), 1)
    cum = jnp.where(r >= c, 1.0, 0.0).astype(BF16)

    first = qi
    t_row = lax.broadcasted_iota(jnp.int32, (rows, win), 0) & (blk - 1)
    s_col = lax.broadcasted_iota(jnp.int32, (rows, win), 1)
    causal = s_col < t_row

    ps = range(pairs)

    def key_start(w):
        return pl.multiple_of(jnp.maximum(w, 0) * win, win)

    def scores(w, slot, p):
        start = key_start(w)
        nz_ref[slot, p] = lax.dot_general(
            qq_ref[p], k_ref[pl.ds(start, win), p * LANES:(p + 1) * LANES],
            (((1,), (1,)), ((), ())), preferred_element_type=F32)

    def weights_times_v(w, slot, p):
        start = key_start(w)
        acc_ref[p] += jnp.dot(w_ref[slot, p],
                              v_ref[pl.ds(start, win), p * LANES:(p + 1) * LANES],
                              preferred_element_type=F32)

    def step(w, slot, diagonal=False):
        incl, carries, top = [], [], None
        for p in ps:
            nz = nz_ref[slot, p]
            sp = jnp.log(1.0 + jnp.exp2(-jnp.abs(nz))) * LOG2E
            log_stay = jnp.minimum(nz, 0.0) - sp
            if diagonal:
                log_stay = jnp.where(causal, log_stay, 0.0)
            incl.append(jnp.dot(log_stay.astype(BF16), cum, preferred_element_type=F32))
            if not diagonal:
                weights_times_v(w + 1, 1 - slot, p)
            scores(w - 1, 1 - slot, p)
            total = incl[p][:, 0:1]
            if diagonal:
                carries.append(None)
            else:
                carries.append(carry_ref[p])
                total = carries[p] + total
            carry_ref[p] = total
            top = total if top is None else jnp.maximum(top, total)
        more_ref[0] = (jnp.max(top) > stop_below).astype(jnp.int32)
        for p in ps:
            if diagonal:
                wt = jnp.where(causal, jnp.exp2(incl[p] - nz_ref[slot, p]), 0.0)
            else:
                wt = jnp.exp2(incl[p] + carries[p] - nz_ref[slot, p])
            w_ref[slot, p] = wt.astype(BF16)

    def unfinished():
        return more_ref[0] == 1

    acc_ref[...] = jnp.zeros_like(acc_ref)
    for p in ps:
        scores(first, 0, p)
    step(first, 0, diagonal=True)

    def body(state):
        w, _, _ = state
        step(w, 1)
        go_second = unfinished() & (w >= 1)

        @pl.when(go_second)
        def _():
            step(w - 1, 0)

        go_on = go_second & (w >= 2) & unfinished()
        return w - 2, go_on, jnp.where(go_second, w - 1, w)

    _, _, last = lax.while_loop(lambda state: state[1], body,
                                (first - 1, (first >= 1) & unfinished(), first))
    for p in ps:
        weights_times_v(last, (first - last) & 1, p)
    for p in range(pairs):
        acc = acc_ref[p]
        o_ref[:, p * LANES:(p + 1) * LANES] = jnp.where(head0, acc[:blk], acc[blk:]).astype(BF16)


def _attention(proj, bsz, seq):
    width = proj.shape[-1]
    proj3 = proj.reshape(bsz, seq, width)
    blk = 256
    pairs = SB_WIDTH // LANES
    return pl.pallas_call(
        functools.partial(_attn_kernel, blk=blk, pairs=pairs),
        grid=(bsz, seq // blk),
        in_specs=[
            pl.BlockSpec((None, blk, SB_WIDTH), lambda b, i: (b, i, 0)),
            pl.BlockSpec((None, seq, SB_WIDTH), lambda b, i: (b, 0, 1)),
            pl.BlockSpec((None, seq, SB_WIDTH), lambda b, i: (b, 0, 2)),
        ],
        out_specs=pl.BlockSpec((None, blk, SB_WIDTH), lambda b, i: (b, i, 0)),
        out_shape=jax.ShapeDtypeStruct((bsz, seq, SB_WIDTH), BF16),
        scratch_shapes=[pltpu.VMEM((pairs, 2 * blk, LANES), BF16),
                        pltpu.VMEM((pairs, 2 * blk, 1), F32),
                        pltpu.VMEM((pairs, 2 * blk, LANES), F32),
                        pltpu.VMEM((2, pairs, 2 * blk, blk), F32),
                        pltpu.VMEM((2, pairs, 2 * blk, blk), BF16),
                        pltpu.SMEM((1,), F32),
                        pltpu.SMEM((1,), jnp.int32)],
        compiler_params=_params(("parallel", "arbitrary")),
        name="sb_attention",
    )(proj3, proj3, proj3).reshape(bsz * seq, SB_WIDTH)


def _gelu(x):
    return 0.5 * x * (1.0 + jnp.tanh(0.7978845608028654 * (x + 0.044715 * (x * x * x))))


def _mix_out_kernel(a_ref, bg_ref, cg_ref, hc_ref, us_ref, vs_ref, cgh_ref, hch_ref,
                    x_ref, gm_ref, cw_ref, cb_ref, ng_ref, sw_ref, sbias_ref, wo_ref,
                    o_ref, uext_ref, *, tm, per_seq, halo):
    i = pl.program_id(0)

    u = cg_ref[...].astype(F32) * hc_ref[...].astype(F32)
    u_halo = cgh_ref[...].astype(F32) * hch_ref[...].astype(F32)
    u_halo = jnp.where(i % per_seq == 0, 0.0, u_halo)
    uext_ref[0:halo, :] = u_halo
    uext_ref[halo:halo + tm, :] = u
    u_m1 = uext_ref[halo - 1:halo - 1 + tm, :]
    u_m2 = uext_ref[halo - 2:halo - 2 + tm, :]
    cw = cw_ref[...]
    y = cw[0:1] * u_m2 + cw[1:2] * u_m1 + cw[2:3] * u + cb_ref[...]
    c_out = bg_ref[...].astype(F32) * y

    ug = _gelu(us_ref[...].astype(F32))
    vg = _gelu(vs_ref[...].astype(F32))
    vn = vg * lax.rsqrt(jnp.mean(vg * vg, axis=-1, keepdims=True) + EPS) * ng_ref[...]
    t_chunk = lax.broadcasted_iota(jnp.int32, (SG_CHUNK, SG_CHUNK), 0) // CHUNK
    s_chunk = lax.broadcasted_iota(jnp.int32, (SG_CHUNK, SG_CHUNK), 1) // CHUNK
    chunk_causal = t_chunk >= s_chunk
    w_cat = jnp.concatenate(
        [jnp.where(chunk_causal, sw_ref[h], 0.0) for h in range(SG_HEADS)], axis=1).astype(BF16)
    lane_head = lax.broadcasted_iota(jnp.int32, (SG_CHUNK, SG_WIDTH), 1) // HEAD_DIM
    sbias = sbias_ref[...]
    mixed = []
    for n in range(tm // SG_CHUNK):
        v_win = vn[n * SG_CHUNK:(n + 1) * SG_CHUNK]
        v_stack = jnp.concatenate(
            [jnp.where(lane_head == h, v_win, 0.0) for h in range(SG_HEADS)], axis=0).astype(BF16)
        mixed.append(jnp.dot(w_cat, v_stack, preferred_element_type=F32) + sbias)
    s_out = ug * jnp.concatenate(mixed, axis=0)

    cat = jnp.concatenate([a_ref[...], c_out.astype(BF16), s_out.astype(BF16)], axis=1)
    mix = jnp.dot(cat, wo_ref[...], preferred_element_type=F32)
    o_ref[...] = x_ref[...] + gm_ref[...] * mix


def _mix_out(a_out, proj, x, mod, conv_w, conv_b, gmlp_norm_g, spatial_w, sbias, w_out, layer, seq):
    n, d = x.shape
    bsz = n // seq
    tm = 512
    per_seq = seq // tm
    halo = BF16_SUBLANES
    col0 = 3 * SB_WIDTH // SC_WIDTH

    def col(k):
        return pl.BlockSpec((tm, SC_WIDTH), lambda i: (i, col0 + k))

    def halo_col(k):
        return pl.BlockSpec((halo, SC_WIDTH),
                            lambda i: (jnp.maximum(i * (tm // halo) - 1, 0), col0 + k))

    return pl.pallas_call(
        functools.partial(_mix_out_kernel, tm=tm, per_seq=per_seq, halo=halo),
        grid=(n // tm,),
        in_specs=[
            pl.BlockSpec((tm, SB_WIDTH), lambda i: (i, 0)),
            col(0), col(1), col(2), col(3), col(4),
            halo_col(1), halo_col(2),
            pl.BlockSpec((tm, d), lambda i: (i, 0)),
            pl.BlockSpec((None, 1, d), lambda i: ((layer * bsz + i // per_seq) * N_MOD + 2, 0, 0)),
            pl.BlockSpec((None, 3, SC_WIDTH), lambda i: (layer, 0, 0)),
            pl.BlockSpec((None, 1, SC_WIDTH), lambda i: (layer, 0, 0)),
            pl.BlockSpec((None, 1, SG_WIDTH), lambda i: (layer, 0, 0)),
            pl.BlockSpec((None, SG_HEADS, SG_CHUNK, SG_CHUNK), lambda i: (layer, 0, 0, 0)),
            pl.BlockSpec((None, SG_CHUNK, SG_WIDTH), lambda i: (layer, 0, 0)),
            pl.BlockSpec((None, d, d), lambda i: (layer, 0, 0)),
        ],
        out_specs=pl.BlockSpec((tm, d), lambda i: (i, 0)),
        out_shape=jax.ShapeDtypeStruct((n, d), F32),
        scratch_shapes=[pltpu.VMEM((halo + tm, SC_WIDTH), F32)],
        compiler_params=_params(("parallel",)),
        name="mix_out",
    )(a_out, proj, proj, proj, proj, proj, proj, proj, x, mod,
      conv_w, conv_b, gmlp_norm_g, spatial_w, sbias, w_out)


def _mlp_kernel(x_ref, g_ref, sc_ref, sh_ref, gf_ref, w1_ref, w2_ref, fg_ref, o_ref,
                h_ref, acc_ref, *, final_norm):
    j = pl.program_id(1)

    @pl.when(j == 0)
    def _():
        h_ref[...] = _modulated_norm(x_ref[...], g_ref[...], sc_ref[...], sh_ref[...]).astype(BF16)
        acc_ref[...] = jnp.zeros_like(acc_ref)

    t = jnp.dot(h_ref[...], w1_ref[...], preferred_element_type=F32)
    t = jnp.square(jnp.maximum(t, 0.0)).astype(BF16)
    acc_ref[...] += jnp.dot(t, w2_ref[...], preferred_element_type=F32)

    @pl.when(j == pl.num_programs(1) - 1)
    def _():
        y = x_ref[...] + gf_ref[...] * acc_ref[...]
        if final_norm:
            y = y * lax.rsqrt(jnp.mean(y * y, axis=-1, keepdims=True) + EPS) * fg_ref[...]
        o_ref[...] = y


def _mlp(x, norm_g, mod, w1, w2, final_g, layer, seq, final_norm):
    n, d = x.shape
    d_ff = w1.shape[-1]
    bsz = n // seq
    tm = 1024
    tf = 512
    per_seq = seq // tm

    def mod_idx(k):
        return lambda i, j: ((layer * bsz + i // per_seq) * N_MOD + k, 0, 0)

    return pl.pallas_call(
        functools.partial(_mlp_kernel, final_norm=final_norm),
        grid=(n // tm, d_ff // tf),
        in_specs=[
            pl.BlockSpec((tm, d), lambda i, j: (i, 0)),
            pl.BlockSpec((None, 1, d), lambda i, j: (layer, 0, 0)),
            pl.BlockSpec((None, 1, d), mod_idx(4)),
            pl.BlockSpec((None, 1, d), mod_idx(3)),
            pl.BlockSpec((None, 1, d), mod_idx(5)),
            pl.BlockSpec((None, d, tf), lambda i, j: (layer, 0, j)),
            pl.BlockSpec((None, tf, d), lambda i, j: (layer, j, 0)),
            pl.BlockSpec((1, d), lambda i, j: (0, 0)),
        ],
        out_specs=pl.BlockSpec((tm, d), lambda i, j: (i, 0)),
        out_shape=jax.ShapeDtypeStruct((n, d), F32),
        scratch_shapes=[pltpu.VMEM((tm, d), BF16), pltpu.VMEM((tm, d), F32)],
        compiler_params=_params(("parallel", "arbitrary")),
        name="mlp",
    )(x, norm_g, mod, mod, mod, w1, w2, final_g)


def kernel(x, c, ada_w, ada_b, norm_mix_g, norm_mlp_g, w_in, conv_w, conv_b, gmlp_norm_g,
           spatial_w, spatial_b, w_out, mlp_w1, mlp_w2, final_norm_g):
    bsz, seq, d = x.shape
    depth = ada_w.shape[0]
    xf = x.reshape(bsz * seq, d)

    mod = _adaln(c, ada_w, ada_b).reshape(depth * bsz * N_MOD, 1, d)
    w_in_b = w_in.astype(BF16)
    w_out_b = w_out.astype(BF16)
    w1_b = mlp_w1.astype(BF16)
    w2_b = mlp_w2.astype(BF16)
    norm_mix = norm_mix_g.reshape(depth, 1, d)
    norm_mlp = norm_mlp_g.reshape(depth, 1, d)
    conv_b3 = conv_b.reshape(depth, 1, SC_WIDTH)
    gmlp_g3 = gmlp_norm_g.reshape(depth, 1, SG_WIDTH)
    sbias = jnp.repeat(jnp.swapaxes(spatial_b, 1, 2), HEAD_DIM, axis=2)
    final_g = final_norm_g.reshape(1, d)

    for layer in range(depth):
        proj = _norm_proj(xf, norm_mix, mod, w_in_b, layer, seq)
        a_out = _attention(proj, bsz, seq)
        xf = _mix_out(a_out, proj, xf, mod, conv_w, conv_b3, gmlp_g3, spatial_w, sbias,
                      w_out_b, layer, seq)
        xf = _mlp(xf, norm_mlp, mod, w1_b, w2_b, final_g, layer, seq,
                  final_norm=(layer == depth - 1))
    return xf.reshape(bsz, seq, d)
```

```python
import functools
import math

import jax
import jax.numpy as jnp
from jax import lax
from jax.experimental import pallas as pl
from jax.experimental.pallas import tpu as pltpu

F32 = jnp.float32
BF16 = jnp.bfloat16

HEAD_DIM = 64
SB_WIDTH = 512
SC_WIDTH = 256
SG_WIDTH = 256
SG_HEADS = 4
SG_CHUNK = 128
CHUNK = 64
N_MOD = 6
EPS = 1e-6
LANES = 128
BF16_SUBLANES = 16
LOG2E = 1.4426950408889634

VMEM_LIMIT = 48 * 1024 * 1024


def _params(sem):
    return pltpu.CompilerParams(dimension_semantics=sem, vmem_limit_bytes=VMEM_LIMIT)


def _adaln_kernel(c_ref, w_ref, b_ref, o_ref):
    c = c_ref[...]
    c_act = (c * jax.nn.sigmoid(c)).astype(BF16)
    o_ref[...] = jnp.dot(c_act, w_ref[...].astype(BF16), preferred_element_type=F32) + b_ref[...]


def _adaln(c, ada_w, ada_b):
    depth, d, width = ada_w.shape
    bsz = c.shape[0]
    tn = 1536
    return pl.pallas_call(
        _adaln_kernel,
        grid=(depth, width // tn),
        in_specs=[
            pl.BlockSpec((bsz, d), lambda l, j: (0, 0)),
            pl.BlockSpec((None, d, tn), lambda l, j: (l, 0, j)),
            pl.BlockSpec((None, 1, tn), lambda l, j: (l, 0, j)),
        ],
        out_specs=pl.BlockSpec((None, bsz, tn), lambda l, j: (l, 0, j)),
        out_shape=jax.ShapeDtypeStruct((depth, bsz, width), F32),
        compiler_params=_params(("parallel", "parallel")),
        name="adaln",
    )(c, ada_w, ada_b.reshape(depth, 1, width))


def _modulated_norm(x, g, sc, sh):
    r = lax.rsqrt(jnp.mean(x * x, axis=-1, keepdims=True) + EPS)
    return x * r * (g * (1.0 + sc)) + sh


def _norm_proj_kernel(x_ref, g_ref, sc_ref, sh_ref, w_ref, o_ref, *, col_chunks):
    h = _modulated_norm(x_ref[...], g_ref[...], sc_ref[...], sh_ref[...]).astype(BF16)
    for lo, hi in col_chunks:
        o_ref[:, lo:hi] = jnp.dot(h, w_ref[:, lo:hi], preferred_element_type=F32).astype(BF16)


def _norm_proj(x, norm_g, mod, w_in, layer, seq):
    n, d = x.shape
    width = w_in.shape[-1]
    bsz = n // seq
    tm = 512
    per_seq = seq // tm
    tn = 512
    col_chunks = tuple((lo, min(lo + tn, width)) for lo in range(0, width, tn))

    def mod_idx(k):
        return lambda i: ((layer * bsz + i // per_seq) * N_MOD + k, 0, 0)

    return pl.pallas_call(
        functools.partial(_norm_proj_kernel, col_chunks=col_chunks),
        grid=(n // tm,),
        in_specs=[
            pl.BlockSpec((tm, d), lambda i: (i, 0)),
            pl.BlockSpec((None, 1, d), lambda i: (layer, 0, 0)),
            pl.BlockSpec((None, 1, d), mod_idx(1)),
            pl.BlockSpec((None, 1, d), mod_idx(0)),
            pl.BlockSpec((None, d, width), lambda i: (layer, 0, 0)),
        ],
        out_specs=pl.BlockSpec((tm, width), lambda i: (i, 0)),
        out_shape=jax.ShapeDtypeStruct((n, width), BF16),
        compiler_params=_params(("parallel",)),
        name="norm_proj",
    )(x, norm_g, mod, mod, w_in)


ZERO_WEIGHT_LOG2 = -140.0


def _attn_kernel(q_ref, k_ref, v_ref, o_ref, qq_ref, carry_ref, acc_ref, nz_ref, w_ref,
                 kmax_ref, more_ref, *, blk, pairs):
    qi = pl.program_id(1)
    win = blk
    rows = 2 * blk
    head0 = lax.broadcasted_iota(jnp.int32, (blk, LANES), 1) < HEAD_DIM
    qscale = -(HEAD_DIM ** -0.5) * LOG2E

    @pl.when(qi == 0)
    def _():
        kmax_ref[0] = jnp.max(jnp.abs(k_ref[...]).astype(F32))

    q_l1 = jnp.zeros((), F32)
    for p in range(pairs):
        q = q_ref[:, p * LANES:(p + 1) * LANES].astype(F32) * qscale
        qq = jnp.concatenate([jnp.where(head0, q, 0.0),
                              jnp.where(head0, 0.0, q)], axis=0).astype(BF16)
        qq_ref[p] = qq
        q_l1 = jnp.maximum(q_l1, jnp.max(jnp.sum(jnp.abs(qq.astype(F32)), axis=1)))
    stop_below = ZERO_WEIGHT_LOG2 - 2.0 ** -8 * (1.01 * q_l1 * kmax_ref[0] + 2.0)

    r = lax.broadcasted_iota(jnp.int32, (win, win), 0)
    c = lax.broadcasted_iota(jnp.int32, (win, win), 1)
    cum = jnp.where(r >= c, 1.0, 0.0).astype(BF16)

    first = qi
    t_row = lax.broadcasted_iota(jnp.int32, (rows, win), 0) & (blk - 1)
    s_col = lax.broadcasted_iota(jnp.int32, (rows, win), 1)
    causal = s_col < t_row

    ps = range(pairs)

    def key_start(w):
        return pl.multiple_of(jnp.maximum(w, 0) * win, win)

    def scores(w, slot, p):
        start = key_start(w)
        nz_ref[slot, p] = lax.dot_general(
            qq_ref[p], k_ref[pl.ds(start, win), p * LANES:(p + 1) * LANES],
            (((1,), (1,)), ((), ())), preferred_element_type=F32)

    def weights_times_v(w, slot, p):
        start = key_start(w)
        acc_ref[p] += jnp.dot(w_ref[slot, p],
                              v_ref[pl.ds(start, win), p * LANES:(p + 1) * LANES],
                              preferred_element_type=F32)

    def step(w, slot, diagonal=False):
        incl, carries, top = [], [], None
        for p in ps:
            nz = nz_ref[slot, p]
            sp = jnp.log(1.0 + jnp.exp2(-jnp.abs(nz))) * LOG2E
            log_stay = jnp.minimum(nz, 0.0) - sp
            if diagonal:
                log_stay = jnp.where(causal, log_stay, 0.0)
            incl.append(jnp.dot(log_stay.astype(BF16), cum, preferred_element_type=F32))
            if not diagonal:
                weights_times_v(w + 1, 1 - slot, p)
            scores(w - 1, 1 - slot, p)
            total = incl[p][:, 0:1]
            if diagonal:
                carries.append(None)
            else:
                carries.append(carry_ref[p])
                total = carries[p] + total
            carry_ref[p] = total
            top = total if top is None else jnp.maximum(top, total)
        more_ref[0] = (jnp.max(top) > stop_below).astype(jnp.int32)
        for p in ps:
            if diagonal:
                wt = jnp.where(causal, jnp.exp2(incl[p] - nz_ref[slot, p]), 0.0)
            else:
                wt = jnp.exp2(incl[p] + carries[p] - nz_ref[slot, p])
            w_ref[slot, p] = wt.astype(BF16)

    def unfinished():
        return more_ref[0] == 1

    acc_ref[...] = jnp.zeros_like(acc_ref)
    for p in ps:
        scores(first, 0, p)
    step(first, 0, diagonal=True)

    def body(state):
        w, _, _ = state
        step(w, 1)
        go_second = unfinished() & (w >= 1)

        @pl.when(go_second)
        def _():
            step(w - 1, 0)

        go_on = go_second & (w >= 2) & unfinished()
        return w - 2, go_on, jnp.where(go_second, w - 1, w)

    _, _, last = lax.while_loop(lambda state: state[1], body,
                                (first - 1, (first >= 1) & unfinished(), first))
    for p in ps:
        weights_times_v(last, (first - last) & 1, p)
    for p in range(pairs):
        acc = acc_ref[p]
        o_ref[:, p * LANES:(p + 1) * LANES] = jnp.where(head0, acc[:blk], acc[blk:]).astype(BF16)


def _attention(proj, bsz, seq):
    width = proj.shape[-1]
    proj3 = proj.reshape(bsz, seq, width)
    blk = 256
    pairs = SB_WIDTH // LANES
    return pl.pallas_call(
        functools.partial(_attn_kernel, blk=blk, pairs=pairs),
        grid=(bsz, seq // blk),
        in_specs=[
            pl.BlockSpec((None, blk, SB_WIDTH), lambda b, i: (b, i, 0)),
            pl.BlockSpec((None, seq, SB_WIDTH), lambda b, i: (b, 0, 1)),
            pl.BlockSpec((None, seq, SB_WIDTH), lambda b, i: (b, 0, 2)),
        ],
        out_specs=pl.BlockSpec((None, blk, SB_WIDTH), lambda b, i: (b, i, 0)),
        out_shape=jax.ShapeDtypeStruct((bsz, seq, SB_WIDTH), BF16),
        scratch_shapes=[pltpu.VMEM((pairs, 2 * blk, LANES), BF16),
                        pltpu.VMEM((pairs, 2 * blk, 1), F32),
                        pltpu.VMEM((pairs, 2 * blk, LANES), F32),
                        pltpu.VMEM((2, pairs, 2 * blk, blk), F32),
                        pltpu.VMEM((2, pairs, 2 * blk, blk), BF16),
                        pltpu.SMEM((1,), F32),
                        pltpu.SMEM((1,), jnp.int32)],
        compiler_params=_params(("parallel", "arbitrary")),
        name="sb_attention",
    )(proj3, proj3, proj3).reshape(bsz * seq, SB_WIDTH)


def _gelu(x):
    return 0.5 * x * (1.0 + jnp.tanh(0.7978845608028654 * (x + 0.044715 * (x * x * x))))


def _mix_out_kernel(a_ref, bg_ref, cg_ref, hc_ref, us_ref, vs_ref, cgh_ref, hch_ref,
                    x_ref, gm_ref, cw_ref, cb_ref, ng_ref, sw_ref, sbias_ref, wo_ref,
                    o_ref, uext_ref, *, tm, per_seq, halo):
    i = pl.program_id(0)

    u = cg_ref[...].astype(F32) * hc_ref[...].astype(F32)
    u_halo = cgh_ref[...].astype(F32) * hch_ref[...].astype(F32)
    u_halo = jnp.where(i % per_seq == 0, 0.0, u_halo)
    uext_ref[0:halo, :] = u_halo
    uext_ref[halo:halo + tm, :] = u
    u_m1 = uext_ref[halo - 1:halo - 1 + tm, :]
    u_m2 = uext_ref[halo - 2:halo - 2 + tm, :]
    cw = cw_ref[...]
    y = cw[0:1] * u_m2 + cw[1:2] * u_m1 + cw[2:3] * u + cb_ref[...]
    c_out = bg_ref[...].astype(F32) * y

    ug = _gelu(us_ref[...].astype(F32))
    vg = _gelu(vs_ref[...].astype(F32))
    vn = vg * lax.rsqrt(jnp.mean(vg * vg, axis=-1, keepdims=True) + EPS) * ng_ref[...]
    t_chunk = lax.broadcasted_iota(jnp.int32, (SG_CHUNK, SG_CHUNK), 0) // CHUNK
    s_chunk = lax.broadcasted_iota(jnp.int32, (SG_CHUNK, SG_CHUNK), 1) // CHUNK
    chunk_causal = t_chunk >= s_chunk
    w_cat = jnp.concatenate(
        [jnp.where(chunk_causal, sw_ref[h], 0.0) for h in range(SG_HEADS)], axis=1).astype(BF16)
    lane_head = lax.broadcasted_iota(jnp.int32, (SG_CHUNK, SG_WIDTH), 1) // HEAD_DIM
    sbias = sbias_ref[...]
    mixed = []
    for n in range(tm // SG_CHUNK):
        v_win = vn[n * SG_CHUNK:(n + 1) * SG_CHUNK]
        v_stack = jnp.concatenate(
            [jnp.where(lane_head == h, v_win, 0.0) for h in range(SG_HEADS)], axis=0).astype(BF16)
        mixed.append(jnp.dot(w_cat, v_stack, preferred_element_type=F32) + sbias)
    s_out = ug * jnp.concatenate(mixed, axis=0)

    cat = jnp.concatenate([a_ref[...], c_out.astype(BF16), s_out.astype(BF16)], axis=1)
    mix = jnp.dot(cat, wo_ref[...], preferred_element_type=F32)
    o_ref[...] = x_ref[...] + gm_ref[...] * mix


def _mix_out(a_out, proj, x, mod, conv_w, conv_b, gmlp_norm_g, spatial_w, sbias, w_out, layer, seq):
    n, d = x.shape
    bsz = n // seq
    tm = 512
    per_seq = seq // tm
    halo = BF16_SUBLANES
    col0 = 3 * SB_WIDTH // SC_WIDTH

    def col(k):
        return pl.BlockSpec((tm, SC_WIDTH), lambda i: (i, col0 + k))

    def halo_col(k):
        return pl.BlockSpec((halo, SC_WIDTH),
                            lambda i: (jnp.maximum(i * (tm // halo) - 1, 0), col0 + k))

    return pl.pallas_call(
        functools.partial(_mix_out_kernel, tm=tm, per_seq=per_seq, halo=halo),
        grid=(n // tm,),
        in_specs=[
            pl.BlockSpec((tm, SB_WIDTH), lambda i: (i, 0)),
            col(0), col(1), col(2), col(3), col(4),
            halo_col(1), halo_col(2),
            pl.BlockSpec((tm, d), lambda i: (i, 0)),
            pl.BlockSpec((None, 1, d), lambda i: ((layer * bsz + i // per_seq) * N_MOD + 2, 0, 0)),
            pl.BlockSpec((None, 3, SC_WIDTH), lambda i: (layer, 0, 0)),
            pl.BlockSpec((None, 1, SC_WIDTH), lambda i: (layer, 0, 0)),
            pl.BlockSpec((None, 1, SG_WIDTH), lambda i: (layer, 0, 0)),
            pl.BlockSpec((None, SG_HEADS, SG_CHUNK, SG_CHUNK), lambda i: (layer, 0, 0, 0)),
            pl.BlockSpec((None, SG_CHUNK, SG_WIDTH), lambda i: (layer, 0, 0)),
            pl.BlockSpec((None, d, d), lambda i: (layer, 0, 0)),
        ],
        out_specs=pl.BlockSpec((tm, d), lambda i: (i, 0)),
        out_shape=jax.ShapeDtypeStruct((n, d), F32),
        scratch_shapes=[pltpu.VMEM((halo + tm, SC_WIDTH), F32)],
        compiler_params=_params(("parallel",)),
        name="mix_out",
    )(a_out, proj, proj, proj, proj, proj, proj, proj, x, mod,
      conv_w, conv_b, gmlp_norm_g, spatial_w, sbias, w_out)


def _mlp_kernel(x_ref, g_ref, sc_ref, sh_ref, gf_ref, w1_ref, w2_ref, fg_ref, o_ref, *,
                final_norm, tf):
    x = x_ref[...]
    h = _modulated_norm(x, g_ref[...], sc_ref[...], sh_ref[...]).astype(BF16)
    acc = None
    for lo in range(0, w1_ref.shape[-1], tf):
        t = jnp.dot(h, w1_ref[:, lo:lo + tf], preferred_element_type=F32)
        t = jnp.square(jnp.maximum(t, 0.0)).astype(BF16)
        part = jnp.dot(t, w2_ref[lo:lo + tf, :], preferred_element_type=F32)
        acc = part if acc is None else acc + part
    y = x + gf_ref[...] * acc
    if final_norm:
        y = y * lax.rsqrt(jnp.mean(y * y, axis=-1, keepdims=True) + EPS) * fg_ref[...]
    o_ref[...] = y


def _mlp(x, norm_g, mod, w1, w2, final_g, layer, seq, final_norm):
    n, d = x.shape
    d_ff = w1.shape[-1]
    bsz = n // seq
    tm = 512
    tf = 512
    per_seq = seq // tm
    resident = pl.Buffered(1)

    def mod_idx(k):
        return lambda i: ((layer * bsz + i // per_seq) * N_MOD + k, 0, 0)

    return pl.pallas_call(
        functools.partial(_mlp_kernel, final_norm=final_norm, tf=tf),
        grid=(n // tm,),
        in_specs=[
            pl.BlockSpec((tm, d), lambda i: (i, 0)),
            pl.BlockSpec((None, 1, d), lambda i: (layer, 0, 0)),
            pl.BlockSpec((None, 1, d), mod_idx(4)),
            pl.BlockSpec((None, 1, d), mod_idx(3)),
            pl.BlockSpec((None, 1, d), mod_idx(5)),
            pl.BlockSpec((None, d, d_ff), lambda i: (layer, 0, 0), pipeline_mode=resident),
            pl.BlockSpec((None, d_ff, d), lambda i: (layer, 0, 0), pipeline_mode=resident),
            pl.BlockSpec((1, d), lambda i: (0, 0)),
        ],
        out_specs=pl.BlockSpec((tm, d), lambda i: (i, 0)),
        out_shape=jax.ShapeDtypeStruct((n, d), F32),
        compiler_params=_params(("parallel",)),
        name="mlp",
    )(x, norm_g, mod, mod, mod, w1, w2, final_g)


def kernel(x, c, ada_w, ada_b, norm_mix_g, norm_mlp_g, w_in, conv_w, conv_b, gmlp_norm_g,
           spatial_w, spatial_b, w_out, mlp_w1, mlp_w2, final_norm_g):
    bsz, seq, d = x.shape
    depth = ada_w.shape[0]
    xf = x.reshape(bsz * seq, d)

    mod = _adaln(c, ada_w, ada_b).reshape(depth * bsz * N_MOD, 1, d)
    w_in_b = w_in.astype(BF16)
    w_out_b = w_out.astype(BF16)
    w1_b = mlp_w1.astype(BF16)
    w2_b = mlp_w2.astype(BF16)
    norm_mix = norm_mix_g.reshape(depth, 1, d)
    norm_mlp = norm_mlp_g.reshape(depth, 1, d)
    conv_b3 = conv_b.reshape(depth, 1, SC_WIDTH)
    gmlp_g3 = gmlp_norm_g.reshape(depth, 1, SG_WIDTH)
    sbias = jnp.repeat(jnp.swapaxes(spatial_b, 1, 2), HEAD_DIM, axis=2)
    final_g = final_norm_g.reshape(1, d)

    for layer in range(depth):
        proj = _norm_proj(xf, norm_mix, mod, w_in_b, layer, seq)
        a_out = _attention(proj, bsz, seq)
        xf = _mix_out(a_out, proj, xf, mod, conv_w, conv_b3, gmlp_g3, spatial_w, sbias,
                      w_out_b, layer, seq)
        xf = _mlp(xf, norm_mlp, mod, w1_b, w2_b, final_g, layer, seq,
                  final_norm=(layer == depth - 1))
    return xf.reshape(bsz, seq, d)
```

```python
import functools
import math

import jax
import jax.numpy as jnp
from jax import lax
from jax.experimental import pallas as pl
from jax.experimental.pallas import tpu as pltpu

F32 = jnp.float32
BF16 = jnp.bfloat16

HEAD_DIM = 64
SB_WIDTH = 512
SC_WIDTH = 256
SG_WIDTH = 256
SG_HEADS = 4
SG_CHUNK = 128
CHUNK = 64
N_MOD = 6
EPS = 1e-6
LANES = 128
BF16_SUBLANES = 16
LOG2E = 1.4426950408889634

VMEM_LIMIT = 48 * 1024 * 1024


def _params(sem):
    return pltpu.CompilerParams(dimension_semantics=sem, vmem_limit_bytes=VMEM_LIMIT)


def _adaln_kernel(c_ref, w_ref, b_ref, o_ref):
    c = c_ref[...]
    c_act = (c * jax.nn.sigmoid(c)).astype(BF16)
    o_ref[...] = jnp.dot(c_act, w_ref[...].astype(BF16), preferred_element_type=F32) + b_ref[...]


def _adaln(c, ada_w, ada_b):
    depth, d, width = ada_w.shape
    bsz = c.shape[0]
    tn = 1536
    return pl.pallas_call(
        _adaln_kernel,
        grid=(depth, width // tn),
        in_specs=[
            pl.BlockSpec((bsz, d), lambda l, j: (0, 0)),
            pl.BlockSpec((None, d, tn), lambda l, j: (l, 0, j)),
            pl.BlockSpec((None, 1, tn), lambda l, j: (l, 0, j)),
        ],
        out_specs=pl.BlockSpec((None, bsz, tn), lambda l, j: (l, 0, j)),
        out_shape=jax.ShapeDtypeStruct((depth, bsz, width), F32),
        compiler_params=_params(("parallel", "parallel")),
        name="adaln",
    )(c, ada_w, ada_b.reshape(depth, 1, width))


def _modulated_norm(x, g, sc, sh):
    r = lax.rsqrt(jnp.mean(x * x, axis=-1, keepdims=True) + EPS)
    return x * r * (g * (1.0 + sc)) + sh


def _norm_proj_kernel(x_ref, g_ref, sc_ref, sh_ref, w_ref, o_ref, *, col_chunks):
    h = _modulated_norm(x_ref[...], g_ref[...], sc_ref[...], sh_ref[...]).astype(BF16)
    for lo, hi in col_chunks:
        o_ref[:, lo:hi] = jnp.dot(h, w_ref[:, lo:hi], preferred_element_type=F32).astype(BF16)


def _norm_proj(x, norm_g, mod, w_in, layer, seq):
    n, d = x.shape
    width = w_in.shape[-1]
    bsz = n // seq
    tm = 512
    per_seq = seq // tm
    tn = 512
    col_chunks = tuple((lo, min(lo + tn, width)) for lo in range(0, width, tn))

    def mod_idx(k):
        return lambda i: ((layer * bsz + i // per_seq) * N_MOD + k, 0, 0)

    return pl.pallas_call(
        functools.partial(_norm_proj_kernel, col_chunks=col_chunks),
        grid=(n // tm,),
        in_specs=[
            pl.BlockSpec((tm, d), lambda i: (i, 0)),
            pl.BlockSpec((None, 1, d), lambda i: (layer, 0, 0)),
            pl.BlockSpec((None, 1, d), mod_idx(1)),
            pl.BlockSpec((None, 1, d), mod_idx(0)),
            pl.BlockSpec((None, d, width), lambda i: (layer, 0, 0)),
        ],
        out_specs=pl.BlockSpec((tm, width), lambda i: (i, 0)),
        out_shape=jax.ShapeDtypeStruct((n, width), BF16),
        compiler_params=_params(("parallel",)),
        name="norm_proj",
    )(x, norm_g, mod, mod, w_in)


ZERO_WEIGHT_LOG2 = -140.0


def _attn_kernel(q_ref, k_ref, v_ref, o_ref, qq_ref, carry_ref, acc_ref, nz_ref, w_ref,
                 kmax_ref, more_ref, *, blk, pairs):
    qi = pl.program_id(1)
    win = blk
    rows = 2 * blk
    head0 = lax.broadcasted_iota(jnp.int32, (blk, LANES), 1) < HEAD_DIM
    qscale = -(HEAD_DIM ** -0.5) * LOG2E

    @pl.when(qi == 0)
    def _():
        kmax_ref[0] = jnp.max(jnp.abs(k_ref[...]).astype(F32))

    q_l1 = jnp.zeros((), F32)
    for p in range(pairs):
        q = q_ref[:, p * LANES:(p + 1) * LANES].astype(F32) * qscale
        qq = jnp.concatenate([jnp.where(head0, q, 0.0),
                              jnp.where(head0, 0.0, q)], axis=0).astype(BF16)
        qq_ref[p] = qq
        q_l1 = jnp.maximum(q_l1, jnp.max(jnp.sum(jnp.abs(qq.astype(F32)), axis=1)))
    stop_below = ZERO_WEIGHT_LOG2 - 2.0 ** -8 * (1.01 * q_l1 * kmax_ref[0] + 2.0)

    r = lax.broadcasted_iota(jnp.int32, (win, win), 0)
    c = lax.broadcasted_iota(jnp.int32, (win, win), 1)
    cum = jnp.where(r >= c, 1.0, 0.0).astype(BF16)

    first = qi
    t_row = lax.broadcasted_iota(jnp.int32, (rows, win), 0) & (blk - 1)
    s_col = lax.broadcasted_iota(jnp.int32, (rows, win), 1)
    causal = s_col < t_row

    ps = range(pairs)

    def key_start(w):
        return pl.multiple_of(jnp.maximum(w, 0) * win, win)

    def scores(w, slot, p):
        start = key_start(w)
        nz_ref[slot, p] = lax.dot_general(
            qq_ref[p], k_ref[pl.ds(start, win), p * LANES:(p + 1) * LANES],
            (((1,), (1,)), ((), ())), preferred_element_type=F32)

    def weights_times_v(w, slot, p):
        start = key_start(w)
        acc_ref[p] += jnp.dot(w_ref[slot, p],
                              v_ref[pl.ds(start, win), p * LANES:(p + 1) * LANES],
                              preferred_element_type=F32)

    def step(w, slot, diagonal=False):
        incl, carries, top = [], [], None
        for p in ps:
            nz = nz_ref[slot, p]
            sp = jnp.log(1.0 + jnp.exp2(-jnp.abs(nz))) * LOG2E
            log_stay = jnp.minimum(nz, 0.0) - sp
            if diagonal:
                log_stay = jnp.where(causal, log_stay, 0.0)
            incl.append(jnp.dot(log_stay.astype(BF16), cum, preferred_element_type=F32))
            if not diagonal:
                weights_times_v(w + 1, 1 - slot, p)
            scores(w - 1, 1 - slot, p)
            total = incl[p][:, 0:1]
            if diagonal:
                carries.append(None)
            else:
                carries.append(carry_ref[p])
                total = carries[p] + total
            carry_ref[p] = total
            top = total if top is None else jnp.maximum(top, total)
        more_ref[0] = (jnp.max(top) > stop_below).astype(jnp.int32)
        for p in ps:
            if diagonal:
                wt = jnp.where(causal, jnp.exp2(incl[p] - nz_ref[slot, p]), 0.0)
            else:
                wt = jnp.exp2(incl[p] + carries[p] - nz_ref[slot, p])
            w_ref[slot, p] = wt.astype(BF16)

    def unfinished():
        return more_ref[0] == 1

    acc_ref[...] = jnp.zeros_like(acc_ref)
    for p in ps:
        scores(first, 0, p)
    step(first, 0, diagonal=True)

    def body(state):
        w, _, _ = state
        step(w, 1)
        go_second = unfinished() & (w >= 1)

        @pl.when(go_second)
        def _():
            step(w - 1, 0)

        go_on = go_second & (w >= 2) & unfinished()
        return w - 2, go_on, jnp.where(go_second, w - 1, w)

    _, _, last = lax.while_loop(lambda state: state[1], body,
                                (first - 1, (first >= 1) & unfinished(), first))
    for p in ps:
        weights_times_v(last, (first - last) & 1, p)
    for p in range(pairs):
        acc = acc_ref[p]
        o_ref[:, p * LANES:(p + 1) * LANES] = jnp.where(head0, acc[:blk], acc[blk:]).astype(BF16)


def _attention(proj, bsz, seq):
    width = proj.shape[-1]
    proj3 = proj.reshape(bsz, seq, width)
    blk = 256
    pairs = SB_WIDTH // LANES
    return pl.pallas_call(
        functools.partial(_attn_kernel, blk=blk, pairs=pairs),
        grid=(bsz, seq // blk),
        in_specs=[
            pl.BlockSpec((None, blk, SB_WIDTH), lambda b, i: (b, i, 0)),
            pl.BlockSpec((None, seq, SB_WIDTH), lambda b, i: (b, 0, 1)),
            pl.BlockSpec((None, seq, SB_WIDTH), lambda b, i: (b, 0, 2)),
        ],
        out_specs=pl.BlockSpec((None, blk, SB_WIDTH), lambda b, i: (b, i, 0)),
        out_shape=jax.ShapeDtypeStruct((bsz, seq, SB_WIDTH), BF16),
        scratch_shapes=[pltpu.VMEM((pairs, 2 * blk, LANES), BF16),
                        pltpu.VMEM((pairs, 2 * blk, 1), F32),
                        pltpu.VMEM((pairs, 2 * blk, LANES), F32),
                        pltpu.VMEM((2, pairs, 2 * blk, blk), F32),
                        pltpu.VMEM((2, pairs, 2 * blk, blk), BF16),
                        pltpu.SMEM((1,), F32),
                        pltpu.SMEM((1,), jnp.int32)],
        compiler_params=_params(("parallel", "arbitrary")),
        name="sb_attention",
    )(proj3, proj3, proj3).reshape(bsz * seq, SB_WIDTH)


def _gelu(x):
    return 0.5 * x * (1.0 + jnp.tanh(0.7978845608028654 * (x + 0.044715 * (x * x * x))))


def _mix_mlp_kernel(a_ref, bg_ref, cg_ref, hc_ref, us_ref, vs_ref, cgh_ref, hch_ref,
                    x_ref, gm_ref, cw_ref, cb_ref, ng_ref, sw_ref, sbias_ref, wo_ref,
                    g_ref, sc_ref, sh_ref, gf_ref, w1_ref, w2_ref, fg_ref,
                    o_ref, uext_ref, *, tm, per_seq, halo, tf, final_norm):
    i = pl.program_id(0)

    u = cg_ref[...].astype(F32) * hc_ref[...].astype(F32)
    u_halo = cgh_ref[...].astype(F32) * hch_ref[...].astype(F32)
    u_halo = jnp.where(i % per_seq == 0, 0.0, u_halo)
    uext_ref[0:halo, :] = u_halo
    uext_ref[halo:halo + tm, :] = u
    u_m1 = uext_ref[halo - 1:halo - 1 + tm, :]
    u_m2 = uext_ref[halo - 2:halo - 2 + tm, :]
    cw = cw_ref[...]
    y = cw[0:1] * u_m2 + cw[1:2] * u_m1 + cw[2:3] * u + cb_ref[...]
    c_out = bg_ref[...].astype(F32) * y

    ug = _gelu(us_ref[...].astype(F32))
    vg = _gelu(vs_ref[...].astype(F32))
    vn = vg * lax.rsqrt(jnp.mean(vg * vg, axis=-1, keepdims=True) + EPS) * ng_ref[...]
    t_chunk = lax.broadcasted_iota(jnp.int32, (SG_CHUNK, SG_CHUNK), 0) // CHUNK
    s_chunk = lax.broadcasted_iota(jnp.int32, (SG_CHUNK, SG_CHUNK), 1) // CHUNK
    chunk_causal = t_chunk >= s_chunk
    w_cat = jnp.concatenate(
        [jnp.where(chunk_causal, sw_ref[h], 0.0) for h in range(SG_HEADS)], axis=1).astype(BF16)
    lane_head = lax.broadcasted_iota(jnp.int32, (SG_CHUNK, SG_WIDTH), 1) // HEAD_DIM
    sbias = sbias_ref[...]
    mixed = []
    for n in range(tm // SG_CHUNK):
        v_win = vn[n * SG_CHUNK:(n + 1) * SG_CHUNK]
        v_stack = jnp.concatenate(
            [jnp.where(lane_head == h, v_win, 0.0) for h in range(SG_HEADS)], axis=0).astype(BF16)
        mixed.append(jnp.dot(w_cat, v_stack, preferred_element_type=F32) + sbias)
    s_out = ug * jnp.concatenate(mixed, axis=0)

    cat = jnp.concatenate([a_ref[...], c_out.astype(BF16), s_out.astype(BF16)], axis=1)
    mix = jnp.dot(cat, wo_ref[...], preferred_element_type=F32)
    x = x_ref[...] + gm_ref[...] * mix

    h = _modulated_norm(x, g_ref[...], sc_ref[...], sh_ref[...]).astype(BF16)
    acc = None
    for lo in range(0, w1_ref.shape[-1], tf):
        t = jnp.dot(h, w1_ref[:, lo:lo + tf], preferred_element_type=F32)
        t = jnp.square(jnp.maximum(t, 0.0)).astype(BF16)
        part = jnp.dot(t, w2_ref[lo:lo + tf, :], preferred_element_type=F32)
        acc = part if acc is None else acc + part
    y = x + gf_ref[...] * acc
    if final_norm:
        y = y * lax.rsqrt(jnp.mean(y * y, axis=-1, keepdims=True) + EPS) * fg_ref[...]
    o_ref[...] = y


def _mix_mlp(a_out, proj, x, mod, conv_w, conv_b, gmlp_norm_g, spatial_w, sbias, w_out,
             norm_g, w1, w2, final_g, layer, seq, final_norm):
    n, d = x.shape
    d_ff = w1.shape[-1]
    bsz = n // seq
    tm = 512
    tf = 512
    per_seq = seq // tm
    halo = BF16_SUBLANES
    col0 = 3 * SB_WIDTH // SC_WIDTH
    resident = pl.Buffered(1)

    def col(k):
        return pl.BlockSpec((tm, SC_WIDTH), lambda i: (i, col0 + k))

    def halo_col(k):
        return pl.BlockSpec((halo, SC_WIDTH),
                            lambda i: (jnp.maximum(i * (tm // halo) - 1, 0), col0 + k))

    def mod_row(k):
        return pl.BlockSpec((None, 1, d),
                            lambda i: ((layer * bsz + i // per_seq) * N_MOD + k, 0, 0))

    def layer_block(*shape):
        zeros = (0,) * len(shape)
        return pl.BlockSpec((None,) + shape, lambda i: (layer,) + zeros, pipeline_mode=resident)

    return pl.pallas_call(
        functools.partial(_mix_mlp_kernel, tm=tm, per_seq=per_seq, halo=halo, tf=tf,
                          final_norm=final_norm),
        grid=(n // tm,),
        in_specs=[
            pl.BlockSpec((tm, SB_WIDTH), lambda i: (i, 0)),
            col(0), col(1), col(2), col(3), col(4),
            halo_col(1), halo_col(2),
            pl.BlockSpec((tm, d), lambda i: (i, 0)),
            mod_row(2),
            layer_block(3, SC_WIDTH),
            layer_block(1, SC_WIDTH),
            layer_block(1, SG_WIDTH),
            layer_block(SG_HEADS, SG_CHUNK, SG_CHUNK),
            layer_block(SG_CHUNK, SG_WIDTH),
            layer_block(d, d),
            layer_block(1, d),
            mod_row(4), mod_row(3), mod_row(5),
            layer_block(d, d_ff),
            layer_block(d_ff, d),
            pl.BlockSpec((1, d), lambda i: (0, 0)),
        ],
        out_specs=pl.BlockSpec((tm, d), lambda i: (i, 0)),
        out_shape=jax.ShapeDtypeStruct((n, d), F32),
        scratch_shapes=[pltpu.VMEM((halo + tm, SC_WIDTH), F32)],
        compiler_params=_params(("parallel",)),
        name="mix_mlp",
    )(a_out, proj, proj, proj, proj, proj, proj, proj, x, mod,
      conv_w, conv_b, gmlp_norm_g, spatial_w, sbias, w_out,
      norm_g, mod, mod, mod, w1, w2, final_g)


def kernel(x, c, ada_w, ada_b, norm_mix_g, norm_mlp_g, w_in, conv_w, conv_b, gmlp_norm_g,
           spatial_w, spatial_b, w_out, mlp_w1, mlp_w2, final_norm_g):
    bsz, seq, d = x.shape
    depth = ada_w.shape[0]
    xf = x.reshape(bsz * seq, d)

    mod = _adaln(c, ada_w, ada_b).reshape(depth * bsz * N_MOD, 1, d)
    w_in_b = w_in.astype(BF16)
    w_out_b = w_out.astype(BF16)
    w1_b = mlp_w1.astype(BF16)
    w2_b = mlp_w2.astype(BF16)
    norm_mix = norm_mix_g.reshape(depth, 1, d)
    norm_mlp = norm_mlp_g.reshape(depth, 1, d)
    conv_b3 = conv_b.reshape(depth, 1, SC_WIDTH)
    gmlp_g3 = gmlp_norm_g.reshape(depth, 1, SG_WIDTH)
    sbias = jnp.repeat(jnp.swapaxes(spatial_b, 1, 2), HEAD_DIM, axis=2)
    final_g = final_norm_g.reshape(1, d)

    for layer in range(depth):
        proj = _norm_proj(xf, norm_mix, mod, w_in_b, layer, seq)
        a_out = _attention(proj, bsz, seq)
        xf = _mix_mlp(a_out, proj, xf, mod, conv_w, conv_b3, gmlp_g3, spatial_w, sbias,
                      w_out_b, norm_mlp, w1_b, w2_b, final_g, layer, seq,
                      final_norm=(layer == depth - 1))
    return xf.reshape(bsz, seq, d)
```

```python
import functools

import jax
import jax.numpy as jnp
from jax import lax
from jax.experimental import pallas as pl
from jax.experimental.pallas import tpu as pltpu

F32 = jnp.float32
BF16 = jnp.bfloat16

HEAD_DIM = 64
SB_WIDTH = 512
SC_WIDTH = 256
SG_WIDTH = 256
SG_HEADS = 4
SG_CHUNK = 128
CHUNK = 64
N_MOD = 6
EPS = 1e-6
LANES = 128
BF16_SUBLANES = 16
LOG2E = 1.4426950408889634

VMEM_LIMIT = 48 * 1024 * 1024


def _params(sem):
    return pltpu.CompilerParams(dimension_semantics=sem, vmem_limit_bytes=VMEM_LIMIT)


def _adaln_kernel(c_ref, w_ref, b_ref, o_ref):
    c = c_ref[...]
    c_act = (c * jax.nn.sigmoid(c)).astype(BF16)
    o_ref[...] = jnp.dot(c_act, w_ref[...].astype(BF16), preferred_element_type=F32) + b_ref[...]


def _adaln(c, ada_w, ada_b):
    depth, d, width = ada_w.shape
    bsz = c.shape[0]
    tn = 1536
    return pl.pallas_call(
        _adaln_kernel,
        grid=(depth, width // tn),
        in_specs=[
            pl.BlockSpec((bsz, d), lambda l, j: (0, 0)),
            pl.BlockSpec((None, d, tn), lambda l, j: (l, 0, j)),
            pl.BlockSpec((None, 1, tn), lambda l, j: (l, 0, j)),
        ],
        out_specs=pl.BlockSpec((None, bsz, tn), lambda l, j: (l, 0, j)),
        out_shape=jax.ShapeDtypeStruct((depth, bsz, width), F32),
        compiler_params=_params(("parallel", "parallel")),
        name="adaln",
    )(c, ada_w, ada_b.reshape(depth, 1, width))


def _modulated_norm(x, g, sc, sh):
    r = lax.rsqrt(jnp.mean(x * x, axis=-1, keepdims=True) + EPS)
    return x * r * (g * (1.0 + sc)) + sh


def _norm_proj_kernel(x_ref, g_ref, sc_ref, sh_ref, w_ref, o_ref, *, sub, col_chunks):
    hs = [_modulated_norm(x_ref[r:r + sub, :], g_ref[...], sc_ref[...], sh_ref[...]).astype(BF16)
          for r in range(0, x_ref.shape[0], sub)]
    for k, h in enumerate(hs):
        for lo, hi in col_chunks:
            o_ref[k * sub:(k + 1) * sub, lo:hi] = jnp.dot(
                h, w_ref[:, lo:hi], preferred_element_type=F32).astype(BF16)


def _norm_proj(x, norm_g, mod, w_in, layer, seq):
    n, d = x.shape
    width = w_in.shape[-1]
    bsz = n // seq
    tm = 1024
    per_seq = seq // tm
    tn = 512
    col_chunks = tuple((lo, min(lo + tn, width)) for lo in range(0, width, tn))

    def mod_idx(k):
        return lambda i: ((layer * bsz + i // per_seq) * N_MOD + k, 0, 0)

    return pl.pallas_call(
        functools.partial(_norm_proj_kernel, sub=256, col_chunks=col_chunks),
        grid=(n // tm,),
        in_specs=[
            pl.BlockSpec((tm, d), lambda i: (i, 0)),
            pl.BlockSpec((None, 1, d), lambda i: (layer, 0, 0)),
            pl.BlockSpec((None, 1, d), mod_idx(1)),
            pl.BlockSpec((None, 1, d), mod_idx(0)),
            pl.BlockSpec((None, d, width), lambda i: (layer, 0, 0)),
        ],
        out_specs=pl.BlockSpec((tm, width), lambda i: (i, 0)),
        out_shape=jax.ShapeDtypeStruct((n, width), BF16),
        compiler_params=_params(("parallel",)),
        name="norm_proj",
    )(x, norm_g, mod, mod, w_in)


ZERO_WEIGHT_LOG2 = -140.0


def _attn_kernel(q_ref, k_ref, v_ref, o_ref, qq_ref, carry_ref, acc_ref, nz_ref, w_ref,
                 kmax_ref, more_ref, *, blk, pairs):
    qi = pl.program_id(1)
    win = blk
    rows = 2 * blk
    head0 = lax.broadcasted_iota(jnp.int32, (blk, LANES), 1) < HEAD_DIM
    qscale = -(HEAD_DIM ** -0.5) * LOG2E

    @pl.when(qi == 0)
    def _():
        kmax_ref[0] = jnp.max(jnp.abs(k_ref[...]).astype(F32))

    q_l1 = jnp.zeros((), F32)
    for p in range(pairs):
        q = q_ref[:, p * LANES:(p + 1) * LANES].astype(F32) * qscale
        qq = jnp.concatenate([jnp.where(head0, q, 0.0),
                              jnp.where(head0, 0.0, q)], axis=0).astype(BF16)
        qq_ref[p] = qq
        q_l1 = jnp.maximum(q_l1, jnp.max(jnp.sum(jnp.abs(qq.astype(F32)), axis=1)))
    stop_below = ZERO_WEIGHT_LOG2 - 2.0 ** -8 * (1.01 * q_l1 * kmax_ref[0] + 2.0)

    r = lax.broadcasted_iota(jnp.int32, (win, win), 0)
    c = lax.broadcasted_iota(jnp.int32, (win, win), 1)
    cum = jnp.where(r >= c, 1.0, 0.0).astype(BF16)

    first = qi
    t_row = lax.broadcasted_iota(jnp.int32, (rows, win), 0) & (blk - 1)
    s_col = lax.broadcasted_iota(jnp.int32, (rows, win), 1)
    causal = s_col < t_row

    ps = range(pairs)

    def key_start(w):
        return pl.multiple_of(jnp.maximum(w, 0) * win, win)

    def scores(w, slot, p):
        start = key_start(w)
        nz_ref[slot, p] = lax.dot_general(
            qq_ref[p], k_ref[pl.ds(start, win), p * LANES:(p + 1) * LANES],
            (((1,), (1,)), ((), ())), preferred_element_type=F32)

    def weights_times_v(w, slot, p):
        start = key_start(w)
        acc_ref[p] += jnp.dot(w_ref[slot, p],
                              v_ref[pl.ds(start, win), p * LANES:(p + 1) * LANES],
                              preferred_element_type=F32)

    def step(w, slot, diagonal=False):
        incl, carries, top = [], [], None
        for p in ps:
            nz = nz_ref[slot, p]
            sp = jnp.log(1.0 + jnp.exp2(-jnp.abs(nz))) * LOG2E
            log_stay = jnp.minimum(nz, 0.0) - sp
            if diagonal:
                log_stay = jnp.where(causal, log_stay, 0.0)
            incl.append(jnp.dot(log_stay.astype(BF16), cum, preferred_element_type=F32))
            if not diagonal:
                weights_times_v(w + 1, 1 - slot, p)
            scores(w - 1, 1 - slot, p)
            total = incl[p][:, 0:1]
            if diagonal:
                carries.append(None)
            else:
                carries.append(carry_ref[p])
                total = carries[p] + total
            carry_ref[p] = total
            top = total if top is None else jnp.maximum(top, total)
        more_ref[0] = (jnp.max(top) > stop_below).astype(jnp.int32)
        for p in ps:
            if diagonal:
                wt = jnp.where(causal, jnp.exp2(incl[p] - nz_ref[slot, p]), 0.0)
            else:
                wt = jnp.exp2(incl[p] + carries[p] - nz_ref[slot, p])
            w_ref[slot, p] = wt.astype(BF16)

    def unfinished():
        return more_ref[0] == 1

    acc_ref[...] = jnp.zeros_like(acc_ref)
    for p in ps:
        scores(first, 0, p)
    step(first, 0, diagonal=True)

    def body(state):
        w, _, _ = state
        step(w, 1)
        go_second = unfinished() & (w >= 1)

        @pl.when(go_second)
        def _():
            step(w - 1, 0)

        go_on = go_second & (w >= 2) & unfinished()
        return w - 2, go_on, jnp.where(go_second, w - 1, w)

    _, _, last = lax.while_loop(lambda state: state[1], body,
                                (first - 1, (first >= 1) & unfinished(), first))
    for p in ps:
        weights_times_v(last, (first - last) & 1, p)
    for p in ps:
        acc = acc_ref[p]
        o_ref[:, p * LANES:(p + 1) * LANES] = jnp.where(head0, acc[:blk], acc[blk:]).astype(BF16)


def _attention(proj, bsz, seq):
    width = proj.shape[-1]
    proj3 = proj.reshape(bsz, seq, width)
    blk = 256
    pairs = SB_WIDTH // LANES
    return pl.pallas_call(
        functools.partial(_attn_kernel, blk=blk, pairs=pairs),
        grid=(bsz, seq // blk),
        in_specs=[
            pl.BlockSpec((None, blk, SB_WIDTH), lambda b, i: (b, i, 0)),
            pl.BlockSpec((None, seq, SB_WIDTH), lambda b, i: (b, 0, 1)),
            pl.BlockSpec((None, seq, SB_WIDTH), lambda b, i: (b, 0, 2)),
        ],
        out_specs=pl.BlockSpec((None, blk, SB_WIDTH), lambda b, i: (b, i, 0)),
        out_shape=jax.ShapeDtypeStruct((bsz, seq, SB_WIDTH), BF16),
        scratch_shapes=[pltpu.VMEM((pairs, 2 * blk, LANES), BF16),
                        pltpu.VMEM((pairs, 2 * blk, 1), F32),
                        pltpu.VMEM((pairs, 2 * blk, LANES), F32),
                        pltpu.VMEM((2, pairs, 2 * blk, blk), F32),
                        pltpu.VMEM((2, pairs, 2 * blk, blk), BF16),
                        pltpu.SMEM((1,), F32),
                        pltpu.SMEM((1,), jnp.int32)],
        compiler_params=_params(("parallel", "arbitrary")),
        name="sb_attention",
    )(proj3, proj3, proj3).reshape(bsz * seq, SB_WIDTH)


def _gelu(x):
    return 0.5 * x * (1.0 + jnp.tanh(0.7978845608028654 * (x + 0.044715 * (x * x * x))))


def _mix_mlp_kernel(a_ref, bg_ref, cg_ref, hc_ref, us_ref, vs_ref, cgh_ref, hch_ref,
                    x_ref, gm_ref, cw_ref, cb_ref, ng_ref, sw_ref, sbias_ref, wo_ref,
                    g_ref, sc_ref, sh_ref, gf_ref, w1_ref, w2_ref, fg_ref,
                    o_ref, uext_ref, *, tm, per_seq, halo, tf, final_norm):
    i = pl.program_id(0)

    u = cg_ref[...].astype(F32) * hc_ref[...].astype(F32)
    u_halo = cgh_ref[...].astype(F32) * hch_ref[...].astype(F32)
    u_halo = jnp.where(i % per_seq == 0, 0.0, u_halo)
    uext_ref[0:halo, :] = u_halo
    uext_ref[halo:halo + tm, :] = u
    u_m1 = uext_ref[halo - 1:halo - 1 + tm, :]
    u_m2 = uext_ref[halo - 2:halo - 2 + tm, :]
    cw = cw_ref[...]
    y = cw[0:1] * u_m2 + cw[1:2] * u_m1 + cw[2:3] * u + cb_ref[...]
    c_out = bg_ref[...].astype(F32) * y

    ug = _gelu(us_ref[...].astype(F32))
    vg = _gelu(vs_ref[...].astype(F32))
    vn = vg * lax.rsqrt(jnp.mean(vg * vg, axis=-1, keepdims=True) + EPS) * ng_ref[...]
    t_chunk = lax.broadcasted_iota(jnp.int32, (SG_CHUNK, SG_CHUNK), 0) // CHUNK
    s_chunk = lax.broadcasted_iota(jnp.int32, (SG_CHUNK, SG_CHUNK), 1) // CHUNK
    chunk_causal = t_chunk >= s_chunk
    w_cat = jnp.concatenate(
        [jnp.where(chunk_causal, sw_ref[h], 0.0) for h in range(SG_HEADS)], axis=1).astype(BF16)
    lane_head = lax.broadcasted_iota(jnp.int32, (SG_CHUNK, SG_WIDTH), 1) // HEAD_DIM
    sbias = sbias_ref[...]
    mixed = []
    for n in range(tm // SG_CHUNK):
        v_win = vn[n * SG_CHUNK:(n + 1) * SG_CHUNK]
        v_stack = jnp.concatenate(
            [jnp.where(lane_head == h, v_win, 0.0) for h in range(SG_HEADS)], axis=0).astype(BF16)
        mixed.append(jnp.dot(w_cat, v_stack, preferred_element_type=F32) + sbias)
    s_out = ug * jnp.concatenate(mixed, axis=0)

    cat = jnp.concatenate([a_ref[...], c_out.astype(BF16), s_out.astype(BF16)], axis=1)
    mix = jnp.dot(cat, wo_ref[...], preferred_element_type=F32)
    x = x_ref[...] + gm_ref[...] * mix

    h = _modulated_norm(x, g_ref[...], sc_ref[...], sh_ref[...]).astype(BF16)
    acc = None
    for lo in range(0, w1_ref.shape[-1], tf):
        t = jnp.dot(h, w1_ref[:, lo:lo + tf], preferred_element_type=F32)
        t = jnp.square(jnp.maximum(t, 0.0)).astype(BF16)
        part = jnp.dot(t, w2_ref[lo:lo + tf, :], preferred_element_type=F32)
        acc = part if acc is None else acc + part
    y = x + gf_ref[...] * acc
    if final_norm:
        y = y * lax.rsqrt(jnp.mean(y * y, axis=-1, keepdims=True) + EPS) * fg_ref[...]
    o_ref[...] = y


def _mix_mlp(a_out, proj, x, mod, conv_w, conv_b, gmlp_norm_g, spatial_w, sbias, w_out,
             norm_g, w1, w2, final_g, layer, seq, final_norm):
    n, d = x.shape
    d_ff = w1.shape[-1]
    bsz = n // seq
    tm = 512
    tf = 512
    per_seq = seq // tm
    halo = BF16_SUBLANES
    col0 = 3 * SB_WIDTH // SC_WIDTH
    resident = pl.Buffered(1)

    def col(k):
        return pl.BlockSpec((tm, SC_WIDTH), lambda i: (i, col0 + k))

    def halo_col(k):
        return pl.BlockSpec((halo, SC_WIDTH),
                            lambda i: (jnp.maximum(i * (tm // halo) - 1, 0), col0 + k))

    def mod_row(k):
        return pl.BlockSpec((None, 1, d),
                            lambda i: ((layer * bsz + i // per_seq) * N_MOD + k, 0, 0))

    def layer_block(*shape):
        zeros = (0,) * len(shape)
        return pl.BlockSpec((None,) + shape, lambda i: (layer,) + zeros, pipeline_mode=resident)

    return pl.pallas_call(
        functools.partial(_mix_mlp_kernel, tm=tm, per_seq=per_seq, halo=halo, tf=tf,
                          final_norm=final_norm),
        grid=(n // tm,),
        in_specs=[
            pl.BlockSpec((tm, SB_WIDTH), lambda i: (i, 0)),
            col(0), col(1), col(2), col(3), col(4),
            halo_col(1), halo_col(2),
            pl.BlockSpec((tm, d), lambda i: (i, 0)),
            mod_row(2),
            layer_block(3, SC_WIDTH),
            layer_block(1, SC_WIDTH),
            layer_block(1, SG_WIDTH),
            layer_block(SG_HEADS, SG_CHUNK, SG_CHUNK),
            layer_block(SG_CHUNK, SG_WIDTH),
            layer_block(d, d),
            layer_block(1, d),
            mod_row(4), mod_row(3), mod_row(5),
            layer_block(d, d_ff),
            layer_block(d_ff, d),
            pl.BlockSpec((1, d), lambda i: (0, 0)),
        ],
        out_specs=pl.BlockSpec((tm, d), lambda i: (i, 0)),
        out_shape=jax.ShapeDtypeStruct((n, d), F32),
        scratch_shapes=[pltpu.VMEM((halo + tm, SC_WIDTH), F32)],
        compiler_params=_params(("parallel",)),
        name="mix_mlp",
    )(a_out, proj, proj, proj, proj, proj, proj, proj, x, mod,
      conv_w, conv_b, gmlp_norm_g, spatial_w, sbias, w_out,
      norm_g, mod, mod, mod, w1, w2, final_g)


def kernel(x, c, ada_w, ada_b, norm_mix_g, norm_mlp_g, w_in, conv_w, conv_b, gmlp_norm_g,
           spatial_w, spatial_b, w_out, mlp_w1, mlp_w2, final_norm_g):
    bsz, seq, d = x.shape
    depth = ada_w.shape[0]
    xf = x.reshape(bsz * seq, d)

    mod = _adaln(c, ada_w, ada_b).reshape(depth * bsz * N_MOD, 1, d)
    w_in_b = w_in.astype(BF16)
    w_out_b = w_out.astype(BF16)
    w1_b = mlp_w1.astype(BF16)
    w2_b = mlp_w2.astype(BF16)
    norm_mix = norm_mix_g.reshape(depth, 1, d)
    norm_mlp = norm_mlp_g.reshape(depth, 1, d)
    conv_b3 = conv_b.reshape(depth, 1, SC_WIDTH)
    gmlp_g3 = gmlp_norm_g.reshape(depth, 1, SG_WIDTH)
    sbias = jnp.repeat(jnp.swapaxes(spatial_b, 1, 2), HEAD_DIM, axis=2)
    final_g = final_norm_g.reshape(1, d)

    for layer in range(depth):
        proj = _norm_proj(xf, norm_mix, mod, w_in_b, layer, seq)
        a_out = _attention(proj, bsz, seq)
        xf = _mix_mlp(a_out, proj, xf, mod, conv_w, conv_b3, gmlp_g3, spatial_w, sbias,
                      w_out_b, norm_mlp, w1_b, w2_b, final_g, layer, seq,
                      final_norm=(layer == depth - 1))
    return xf.reshape(bsz, seq, d)
```

```python
import functools

import jax
import jax.numpy as jnp
from jax import lax
from jax.experimental import pallas as pl
from jax.experimental.pallas import tpu as pltpu

F32 = jnp.float32
BF16 = jnp.bfloat16

HEAD_DIM = 64
SB_WIDTH = 512
SC_WIDTH = 256
SG_WIDTH = 256
SG_HEADS = 4
SG_CHUNK = 128
CHUNK = 64
N_MOD = 6
EPS = 1e-6
LANES = 128
BF16_SUBLANES = 16
LOG2E = 1.4426950408889634

VMEM_LIMIT = 48 * 1024 * 1024


def _params(sem):
    return pltpu.CompilerParams(dimension_semantics=sem, vmem_limit_bytes=VMEM_LIMIT)


def _adaln_kernel(c_ref, w_ref, b_ref, o_ref):
    c = c_ref[...]
    c_act = (c * jax.nn.sigmoid(c)).astype(BF16)
    o_ref[...] = jnp.dot(c_act, w_ref[...].astype(BF16), preferred_element_type=F32) + b_ref[...]


def _adaln(c, ada_w, ada_b):
    depth, d, width = ada_w.shape
    bsz = c.shape[0]
    tn = 1536
    return pl.pallas_call(
        _adaln_kernel,
        grid=(depth, width // tn),
        in_specs=[
            pl.BlockSpec((bsz, d), lambda l, j: (0, 0)),
            pl.BlockSpec((None, d, tn), lambda l, j: (l, 0, j)),
            pl.BlockSpec((None, 1, tn), lambda l, j: (l, 0, j)),
        ],
        out_specs=pl.BlockSpec((None, bsz, tn), lambda l, j: (l, 0, j)),
        out_shape=jax.ShapeDtypeStruct((depth, bsz, width), F32),
        compiler_params=_params(("parallel", "parallel")),
        name="adaln",
    )(c, ada_w, ada_b.reshape(depth, 1, width))


def _modulated_norm(x, g, sc, sh):
    r = lax.rsqrt(jnp.mean(x * x, axis=-1, keepdims=True) + EPS)
    return x * r * (g * (1.0 + sc)) + sh


def _norm_proj_kernel(x_ref, g_ref, sc_ref, sh_ref, w_ref, o_ref, *, sub, col_chunks):
    hs = [_modulated_norm(x_ref[r:r + sub, :], g_ref[...], sc_ref[...], sh_ref[...]).astype(BF16)
          for r in range(0, x_ref.shape[0], sub)]
    for k, h in enumerate(hs):
        for lo, hi in col_chunks:
            o_ref[k * sub:(k + 1) * sub, lo:hi] = jnp.dot(
                h, w_ref[:, lo:hi], preferred_element_type=F32).astype(BF16)


def _norm_proj(x, norm_g, mod, w_in, layer, seq):
    n, d = x.shape
    width = w_in.shape[-1]
    bsz = n // seq
    tm = 1024
    per_seq = seq // tm
    tn = 512
    col_chunks = tuple((lo, min(lo + tn, width)) for lo in range(0, width, tn))

    def mod_idx(k):
        return lambda i: ((layer * bsz + i // per_seq) * N_MOD + k, 0, 0)

    return pl.pallas_call(
        functools.partial(_norm_proj_kernel, sub=256, col_chunks=col_chunks),
        grid=(n // tm,),
        in_specs=[
            pl.BlockSpec((tm, d), lambda i: (i, 0)),
            pl.BlockSpec((None, 1, d), lambda i: (layer, 0, 0)),
            pl.BlockSpec((None, 1, d), mod_idx(1)),
            pl.BlockSpec((None, 1, d), mod_idx(0)),
            pl.BlockSpec((None, d, width), lambda i: (layer, 0, 0)),
        ],
        out_specs=pl.BlockSpec((tm, width), lambda i: (i, 0)),
        out_shape=jax.ShapeDtypeStruct((n, width), BF16),
        compiler_params=_params(("parallel",)),
        name="norm_proj",
    )(x, norm_g, mod, mod, w_in)


ZERO_WEIGHT_LOG2 = -140.0


def _attn_kernel(q_ref, k_ref, v_ref, o_ref, qq_ref, carry_ref, acc_ref, nz_ref, w_ref,
                 more_ref, *, blk, pairs):
    win = blk
    rows = 2 * blk
    n_blocks = q_ref.shape[0] // blk
    head0 = lax.broadcasted_iota(jnp.int32, (blk, LANES), 1) < HEAD_DIM
    qscale = -(HEAD_DIM ** -0.5) * LOG2E
    k_max = jnp.max(jnp.max(jnp.abs(k_ref[...]), axis=0, keepdims=True).astype(F32))

    r = lax.broadcasted_iota(jnp.int32, (win, win), 0)
    c = lax.broadcasted_iota(jnp.int32, (win, win), 1)
    cum = jnp.where(r >= c, 1.0, 0.0).astype(BF16)

    t_row = lax.broadcasted_iota(jnp.int32, (rows, win), 0) & (blk - 1)
    s_col = lax.broadcasted_iota(jnp.int32, (rows, win), 1)
    causal = s_col < t_row
    ps = range(pairs)

    def key_start(w):
        return pl.multiple_of(jnp.maximum(w, 0) * win, win)

    def last_weights_times_v(last, slot, buf):
        start = key_start(last)
        for p in ps:
            acc_ref[buf, p] += jnp.dot(
                w_ref[slot, p], v_ref[pl.ds(start, win), p * LANES:(p + 1) * LANES],
                preferred_element_type=F32)

    def write_output(block, buf):
        out_rows = pl.ds(pl.multiple_of(block * blk, blk), blk)
        for p in ps:
            acc = acc_ref[buf, p]
            o_ref[out_rows, p * LANES:(p + 1) * LANES] = jnp.where(
                head0, acc[:blk], acc[blk:]).astype(BF16)

    def query_block(first, buf, pending):
        q_rows = pl.ds(pl.multiple_of(first * blk, blk), blk)
        q_l1 = jnp.zeros((), F32)
        for p in ps:
            q = q_ref[q_rows, p * LANES:(p + 1) * LANES].astype(F32) * qscale
            qq = jnp.concatenate([jnp.where(head0, q, 0.0),
                                  jnp.where(head0, 0.0, q)], axis=0).astype(BF16)
            qq_ref[p] = qq
            q_l1 = jnp.maximum(q_l1, jnp.max(jnp.sum(jnp.abs(qq.astype(F32)), axis=1)))
        stop_below = ZERO_WEIGHT_LOG2 - 2.0 ** -8 * (1.01 * q_l1 * k_max + 2.0)

        def scores(w, slot, p):
            start = key_start(w)
            nz_ref[slot, p] = lax.dot_general(
                qq_ref[p], k_ref[pl.ds(start, win), p * LANES:(p + 1) * LANES],
                (((1,), (1,)), ((), ())), preferred_element_type=F32)

        def weights_times_v(w, slot, p):
            start = key_start(w)
            acc_ref[buf, p] += jnp.dot(w_ref[slot, p],
                                       v_ref[pl.ds(start, win), p * LANES:(p + 1) * LANES],
                                       preferred_element_type=F32)

        def step(w, slot, diagonal=False):
            incl, carries, top = [], [], None
            for p in ps:
                nz = nz_ref[slot, p]
                sp = jnp.log(1.0 + jnp.exp2(-jnp.abs(nz))) * LOG2E
                log_stay = jnp.minimum(nz, 0.0) - sp
                if diagonal:
                    log_stay = jnp.where(causal, log_stay, 0.0)
                incl.append(jnp.dot(log_stay.astype(BF16), cum, preferred_element_type=F32))
                if not diagonal:
                    weights_times_v(w + 1, 1 - slot, p)
                scores(w - 1, 1 - slot, p)
                total = incl[p][:, 0:1]
                if diagonal:
                    carries.append(None)
                else:
                    carries.append(carry_ref[p])
                    total = carries[p] + total
                carry_ref[p] = total
                top = total if top is None else jnp.maximum(top, total)
            more_ref[0] = (jnp.max(top) > stop_below).astype(jnp.int32)
            for p in ps:
                if diagonal:
                    wt = jnp.where(causal, jnp.exp2(incl[p] - nz_ref[slot, p]), 0.0)
                else:
                    wt = jnp.exp2(incl[p] + carries[p] - nz_ref[slot, p])
                w_ref[slot, p] = wt.astype(BF16)

        def unfinished():
            return more_ref[0] == 1

        prev_block, prev_last, prev_slot = pending
        acc_ref[buf] = jnp.zeros(acc_ref.shape[1:], F32)
        for p in ps:
            scores(first, 0, p)
        last_weights_times_v(prev_last, prev_slot, 1 - buf)
        step(first, 0, diagonal=True)
        write_output(prev_block, 1 - buf)

        def body(state):
            w, _, _ = state
            step(w, 1)
            go_second = unfinished() & (w >= 1)

            @pl.when(go_second)
            def _():
                step(w - 1, 0)

            go_on = go_second & (w >= 2) & unfinished()
            return w - 2, go_on, jnp.where(go_second, w - 1, w)

        _, _, last = lax.while_loop(lambda state: state[1], body,
                                    (first - 1, (first >= 1) & unfinished(), first))
        return first, last, (first - last) & 1

    w_ref[...] = jnp.zeros_like(w_ref)
    acc_ref[...] = jnp.zeros_like(acc_ref)

    def two_blocks(m, pending):
        pending = query_block(2 * m, 0, pending)
        return query_block(2 * m + 1, 1, pending)

    zero = jnp.zeros((), jnp.int32)
    block, last, slot = lax.fori_loop(0, n_blocks // 2, two_blocks, (zero, zero, zero))
    last_weights_times_v(last, slot, 1)
    write_output(block, 1)


def _attention(proj, bsz, seq):
    width = proj.shape[-1]
    proj3 = proj.reshape(bsz, seq, width)
    blk = 256
    pairs = SB_WIDTH // LANES

    def column_block(k):
        return pl.BlockSpec((None, seq, SB_WIDTH), lambda b: (b, 0, k))

    return pl.pallas_call(
        functools.partial(_attn_kernel, blk=blk, pairs=pairs),
        grid=(bsz,),
        in_specs=[column_block(0), column_block(1), column_block(2)],
        out_specs=pl.BlockSpec((None, seq, SB_WIDTH), lambda b: (b, 0, 0)),
        out_shape=jax.ShapeDtypeStruct((bsz, seq, SB_WIDTH), BF16),
        scratch_shapes=[pltpu.VMEM((pairs, 2 * blk, LANES), BF16),
                        pltpu.VMEM((pairs, 2 * blk, 1), F32),
                        pltpu.VMEM((2, pairs, 2 * blk, LANES), F32),
                        pltpu.VMEM((2, pairs, 2 * blk, blk), F32),
                        pltpu.VMEM((2, pairs, 2 * blk, blk), BF16),
                        pltpu.SMEM((1,), jnp.int32)],
        compiler_params=_params(("parallel",)),
        name="sb_attention",
    )(proj3, proj3, proj3).reshape(bsz * seq, SB_WIDTH)


def _gelu(x):
    return 0.5 * x * (1.0 + jnp.tanh(0.7978845608028654 * (x + 0.044715 * (x * x * x))))


def _mix_mlp_kernel(a_ref, bg_ref, cg_ref, hc_ref, us_ref, vs_ref, cgh_ref, hch_ref,
                    x_ref, gm_ref, cw_ref, cb_ref, ng_ref, sw_ref, sbias_ref, wo_ref,
                    g_ref, sc_ref, sh_ref, gf_ref, w1_ref, w2_ref, fg_ref,
                    o_ref, uext_ref, *, tm, per_seq, halo, tf, final_norm):
    i = pl.program_id(0)

    u = cg_ref[...].astype(F32) * hc_ref[...].astype(F32)
    u_halo = cgh_ref[...].astype(F32) * hch_ref[...].astype(F32)
    u_halo = jnp.where(i % per_seq == 0, 0.0, u_halo)
    uext_ref[0:halo, :] = u_halo
    uext_ref[halo:halo + tm, :] = u
    u_m1 = uext_ref[halo - 1:halo - 1 + tm, :]
    u_m2 = uext_ref[halo - 2:halo - 2 + tm, :]
    cw = cw_ref[...]
    y = cw[0:1] * u_m2 + cw[1:2] * u_m1 + cw[2:3] * u + cb_ref[...]
    c_out = bg_ref[...].astype(F32) * y

    ug = _gelu(us_ref[...].astype(F32))
    vg = _gelu(vs_ref[...].astype(F32))
    vn = vg * lax.rsqrt(jnp.mean(vg * vg, axis=-1, keepdims=True) + EPS) * ng_ref[...]
    t_chunk = lax.broadcasted_iota(jnp.int32, (SG_CHUNK, SG_CHUNK), 0) // CHUNK
    s_chunk = lax.broadcasted_iota(jnp.int32, (SG_CHUNK, SG_CHUNK), 1) // CHUNK
    chunk_causal = t_chunk >= s_chunk
    w_cat = jnp.concatenate(
        [jnp.where(chunk_causal, sw_ref[h], 0.0) for h in range(SG_HEADS)], axis=1).astype(BF16)
    lane_head = lax.broadcasted_iota(jnp.int32, (SG_CHUNK, SG_WIDTH), 1) // HEAD_DIM
    sbias = sbias_ref[...]
    mixed = []
    for n in range(tm // SG_CHUNK):
        v_win = vn[n * SG_CHUNK:(n + 1) * SG_CHUNK]
        v_stack = jnp.concatenate(
            [jnp.where(lane_head == h, v_win, 0.0) for h in range(SG_HEADS)], axis=0).astype(BF16)
        mixed.append(jnp.dot(w_cat, v_stack, preferred_element_type=F32) + sbias)
    s_out = ug * jnp.concatenate(mixed, axis=0)

    cat = jnp.concatenate([a_ref[...], c_out.astype(BF16), s_out.astype(BF16)], axis=1)
    mix = jnp.dot(cat, wo_ref[...], preferred_element_type=F32)
    x = x_ref[...] + gm_ref[...] * mix

    h = _modulated_norm(x, g_ref[...], sc_ref[...], sh_ref[...]).astype(BF16)
    acc = None
    for lo in range(0, w1_ref.shape[-1], tf):
        t = jnp.dot(h, w1_ref[:, lo:lo + tf], preferred_element_type=F32)
        t = jnp.square(jnp.maximum(t, 0.0)).astype(BF16)
        part = jnp.dot(t, w2_ref[lo:lo + tf, :], preferred_element_type=F32)
        acc = part if acc is None else acc + part
    y = x + gf_ref[...] * acc
    if final_norm:
        y = y * lax.rsqrt(jnp.mean(y * y, axis=-1, keepdims=True) + EPS) * fg_ref[...]
    o_ref[...] = y


def _mix_mlp(a_out, proj, x, mod, conv_w, conv_b, gmlp_norm_g, spatial_w, sbias, w_out,
             norm_g, w1, w2, final_g, layer, seq, final_norm):
    n, d = x.shape
    d_ff = w1.shape[-1]
    bsz = n // seq
    tm = 512
    tf = 512
    per_seq = seq // tm
    halo = BF16_SUBLANES
    col0 = 3 * SB_WIDTH // SC_WIDTH
    resident = pl.Buffered(1)

    def col(k):
        return pl.BlockSpec((tm, SC_WIDTH), lambda i: (i, col0 + k))

    def halo_col(k):
        return pl.BlockSpec((halo, SC_WIDTH),
                            lambda i: (jnp.maximum(i * (tm // halo) - 1, 0), col0 + k))

    def mod_row(k):
        return pl.BlockSpec((None, 1, d),
                            lambda i: ((layer * bsz + i // per_seq) * N_MOD + k, 0, 0))

    def layer_block(*shape):
        zeros = (0,) * len(shape)
        return pl.BlockSpec((None,) + shape, lambda i: (layer,) + zeros, pipeline_mode=resident)

    return pl.pallas_call(
        functools.partial(_mix_mlp_kernel, tm=tm, per_seq=per_seq, halo=halo, tf=tf,
                          final_norm=final_norm),
        grid=(n // tm,),
        in_specs=[
            pl.BlockSpec((tm, SB_WIDTH), lambda i: (i, 0)),
            col(0), col(1), col(2), col(3), col(4),
            halo_col(1), halo_col(2),
            pl.BlockSpec((tm, d), lambda i: (i, 0)),
            mod_row(2),
            layer_block(3, SC_WIDTH),
            layer_block(1, SC_WIDTH),
            layer_block(1, SG_WIDTH),
            layer_block(SG_HEADS, SG_CHUNK, SG_CHUNK),
            layer_block(SG_CHUNK, SG_WIDTH),
            layer_block(d, d),
            layer_block(1, d),
            mod_row(4), mod_row(3), mod_row(5),
            layer_block(d, d_ff),
            layer_block(d_ff, d),
            pl.BlockSpec((1, d), lambda i: (0, 0)),
        ],
        out_specs=pl.BlockSpec((tm, d), lambda i: (i, 0)),
        out_shape=jax.ShapeDtypeStruct((n, d), F32),
        scratch_shapes=[pltpu.VMEM((halo + tm, SC_WIDTH), F32)],
        compiler_params=_params(("parallel",)),
        name="mix_mlp",
    )(a_out, proj, proj, proj, proj, proj, proj, proj, x, mod,
      conv_w, conv_b, gmlp_norm_g, spatial_w, sbias, w_out,
      norm_g, mod, mod, mod, w1, w2, final_g)


def kernel(x, c, ada_w, ada_b, norm_mix_g, norm_mlp_g, w_in, conv_w, conv_b, gmlp_norm_g,
           spatial_w, spatial_b, w_out, mlp_w1, mlp_w2, final_norm_g):
    bsz, seq, d = x.shape
    depth = ada_w.shape[0]
    xf = x.reshape(bsz * seq, d)

    mod = _adaln(c, ada_w, ada_b).reshape(depth * bsz * N_MOD, 1, d)
    w_in_b = w_in.astype(BF16)
    w_out_b = w_out.astype(BF16)
    w1_b = mlp_w1.astype(BF16)
    w2_b = mlp_w2.astype(BF16)
    norm_mix = norm_mix_g.reshape(depth, 1, d)
    norm_mlp = norm_mlp_g.reshape(depth, 1, d)
    conv_b3 = conv_b.reshape(depth, 1, SC_WIDTH)
    gmlp_g3 = gmlp_norm_g.reshape(depth, 1, SG_WIDTH)
    sbias = jnp.repeat(jnp.swapaxes(spatial_b, 1, 2), HEAD_DIM, axis=2)
    final_g = final_norm_g.reshape(1, d)

    for layer in range(depth):
        proj = _norm_proj(xf, norm_mix, mod, w_in_b, layer, seq)
        a_out = _attention(proj, bsz, seq)
        xf = _mix_mlp(a_out, proj, xf, mod, conv_w, conv_b3, gmlp_g3, spatial_w, sbias,
                      w_out_b, norm_mlp, w1_b, w2_b, final_g, layer, seq,
                      final_norm=(layer == depth - 1))
    return xf.reshape(bsz, seq, d)
```

```python
import functools

import jax
import jax.numpy as jnp
from jax import lax
from jax.experimental import pallas as pl
from jax.experimental.pallas import tpu as pltpu

F32 = jnp.float32
BF16 = jnp.bfloat16

HEAD_DIM = 64
SB_WIDTH = 512
SC_WIDTH = 256
SG_WIDTH = 256
SG_HEADS = 4
SG_CHUNK = 128
CHUNK = 64
N_MOD = 6
EPS = 1e-6
LANES = 128
BF16_SUBLANES = 16
LOG2E = 1.4426950408889634

VMEM_LIMIT = 48 * 1024 * 1024


def _params(sem):
    return pltpu.CompilerParams(dimension_semantics=sem, vmem_limit_bytes=VMEM_LIMIT)


def _adaln_kernel(c_ref, w_ref, b_ref, o_ref):
    c = c_ref[...]
    c_act = (c * jax.nn.sigmoid(c)).astype(BF16)
    o_ref[...] = jnp.dot(c_act, w_ref[...].astype(BF16), preferred_element_type=F32) + b_ref[...]


def _adaln(c, ada_w, ada_b):
    depth, d, width = ada_w.shape
    bsz = c.shape[0]
    tn = 1536
    return pl.pallas_call(
        _adaln_kernel,
        grid=(depth, width // tn),
        in_specs=[
            pl.BlockSpec((bsz, d), lambda l, j: (0, 0)),
            pl.BlockSpec((None, d, tn), lambda l, j: (l, 0, j)),
            pl.BlockSpec((None, 1, tn), lambda l, j: (l, 0, j)),
        ],
        out_specs=pl.BlockSpec((None, bsz, tn), lambda l, j: (l, 0, j)),
        out_shape=jax.ShapeDtypeStruct((depth, bsz, width), F32),
        compiler_params=_params(("parallel", "parallel")),
        name="adaln",
    )(c, ada_w, ada_b.reshape(depth, 1, width))


def _modulated_norm(x, g, sc, sh):
    r = lax.rsqrt(jnp.mean(x * x, axis=-1, keepdims=True) + EPS)
    return x * r * (g * (1.0 + sc)) + sh


def _norm_proj_kernel(x_ref, g_ref, sc_ref, sh_ref, w_ref, o_ref, *, sub, col_chunks):
    hs = [_modulated_norm(x_ref[r:r + sub, :], g_ref[...], sc_ref[...], sh_ref[...]).astype(BF16)
          for r in range(0, x_ref.shape[0], sub)]
    for k, h in enumerate(hs):
        for lo, hi in col_chunks:
            o_ref[k * sub:(k + 1) * sub, lo:hi] = jnp.dot(
                h, w_ref[:, lo:hi], preferred_element_type=F32).astype(BF16)


def _norm_proj(x, norm_g, mod, w_in, layer, seq):
    n, d = x.shape
    width = w_in.shape[-1]
    bsz = n // seq
    tm = 1024
    per_seq = seq // tm
    tn = 512
    col_chunks = tuple((lo, min(lo + tn, width)) for lo in range(0, width, tn))

    def mod_idx(k):
        return lambda i: ((layer * bsz + i // per_seq) * N_MOD + k, 0, 0)

    return pl.pallas_call(
        functools.partial(_norm_proj_kernel, sub=256, col_chunks=col_chunks),
        grid=(n // tm,),
        in_specs=[
            pl.BlockSpec((tm, d), lambda i: (i, 0)),
            pl.BlockSpec((None, 1, d), lambda i: (layer, 0, 0)),
            pl.BlockSpec((None, 1, d), mod_idx(1)),
            pl.BlockSpec((None, 1, d), mod_idx(0)),
            pl.BlockSpec((None, d, width), lambda i: (layer, 0, 0)),
        ],
        out_specs=pl.BlockSpec((tm, width), lambda i: (i, 0)),
        out_shape=jax.ShapeDtypeStruct((n, width), BF16),
        compiler_params=_params(("parallel",)),
        name="norm_proj",
    )(x, norm_g, mod, mod, w_in)


ZERO_WEIGHT_LOG2 = -140.0


def _attn_kernel(q_ref, k_ref, v_ref, o_ref, qq_ref, carry_ref, acc_ref, nz_ref, w_ref,
                 more_ref, *, blk, pairs):
    win = blk
    rows = 2 * blk
    n_blocks = q_ref.shape[0] // blk
    head0 = lax.broadcasted_iota(jnp.int32, (blk, LANES), 1) < HEAD_DIM
    qscale = -(HEAD_DIM ** -0.5) * LOG2E
    k_max = jnp.max(jnp.max(jnp.abs(k_ref[...]), axis=0, keepdims=True).astype(F32))

    r = lax.broadcasted_iota(jnp.int32, (win, win), 0)
    c = lax.broadcasted_iota(jnp.int32, (win, win), 1)
    cum = jnp.where(r >= c, 1.0, 0.0).astype(BF16)

    t_row = lax.broadcasted_iota(jnp.int32, (rows, win), 0) & (blk - 1)
    s_col = lax.broadcasted_iota(jnp.int32, (rows, win), 1)
    causal = s_col < t_row
    ps = range(pairs)

    def key_start(w):
        return pl.multiple_of(jnp.maximum(w, 0) * win, win)

    def last_weights_times_v(last, slot, buf):
        start = key_start(last)
        for p in ps:
            acc_ref[buf, p] += jnp.dot(
                w_ref[slot, p], v_ref[pl.ds(start, win), p * LANES:(p + 1) * LANES],
                preferred_element_type=F32)

    def write_output(block, buf):
        out_rows = pl.ds(pl.multiple_of(block * blk, blk), blk)
        for p in ps:
            acc = acc_ref[buf, p]
            o_ref[out_rows, p * LANES:(p + 1) * LANES] = jnp.where(
                head0, acc[:blk], acc[blk:]).astype(BF16)

    def query_block(first, buf, pending):
        q_rows = pl.ds(pl.multiple_of(first * blk, blk), blk)
        q_l1 = jnp.zeros((), F32)
        for p in ps:
            q = q_ref[q_rows, p * LANES:(p + 1) * LANES].astype(F32) * qscale
            qq = jnp.concatenate([jnp.where(head0, q, 0.0),
                                  jnp.where(head0, 0.0, q)], axis=0).astype(BF16)
            qq_ref[p] = qq
            q_l1 = jnp.maximum(q_l1, jnp.max(jnp.sum(jnp.abs(qq.astype(F32)), axis=1)))
        stop_below = ZERO_WEIGHT_LOG2 - 2.0 ** -8 * (1.01 * q_l1 * k_max + 2.0)

        def scores(w, slot, p):
            start = key_start(w)
            nz_ref[slot, p] = lax.dot_general(
                qq_ref[p], k_ref[pl.ds(start, win), p * LANES:(p + 1) * LANES],
                (((1,), (1,)), ((), ())), preferred_element_type=F32)

        def weights_times_v(w, slot, p):
            start = key_start(w)
            acc_ref[buf, p] += jnp.dot(w_ref[slot, p],
                                       v_ref[pl.ds(start, win), p * LANES:(p + 1) * LANES],
                                       preferred_element_type=F32)

        def step(w, slot, diagonal=False):
            incl, carries, top = [], [], None
            for p in ps:
                nz = nz_ref[slot, p]
                sp = jnp.log(1.0 + jnp.exp2(-jnp.abs(nz))) * LOG2E
                log_stay = jnp.minimum(nz, 0.0) - sp
                if diagonal:
                    log_stay = jnp.where(causal, log_stay, 0.0)
                incl.append(jnp.dot(log_stay.astype(BF16), cum, preferred_element_type=F32))
                if not diagonal:
                    weights_times_v(w + 1, 1 - slot, p)
                scores(w - 1, 1 - slot, p)
                total = incl[p][:, 0:1]
                if diagonal:
                    carries.append(None)
                else:
                    carries.append(carry_ref[p])
                    total = carries[p] + total
                carry_ref[p] = total
                top = total if top is None else jnp.maximum(top, total)
            more_ref[0] = (jnp.max(top) > stop_below).astype(jnp.int32)
            for p in ps:
                if diagonal:
                    wt = jnp.where(causal, jnp.exp2(incl[p] - nz_ref[slot, p]), 0.0)
                else:
                    wt = jnp.exp2(incl[p] + carries[p] - nz_ref[slot, p])
                w_ref[slot, p] = wt.astype(BF16)

        def unfinished():
            return more_ref[0] == 1

        prev_block, prev_last, prev_slot = pending
        acc_ref[buf] = jnp.zeros(acc_ref.shape[1:], F32)
        for p in ps:
            scores(first, 0, p)
        last_weights_times_v(prev_last, prev_slot, 1 - buf)
        step(first, 0, diagonal=True)
        write_output(prev_block, 1 - buf)

        def body(state):
            w, _, _ = state
            step(w, 1)
            go_second = unfinished() & (w >= 1)

            @pl.when(go_second)
            def _():
                step(w - 1, 0)

            go_on = go_second & (w >= 2) & unfinished()
            return w - 2, go_on, jnp.where(go_second, w - 1, w)

        _, _, last = lax.while_loop(lambda state: state[1], body,
                                    (first - 1, (first >= 1) & unfinished(), first))
        return first, last, (first - last) & 1

    w_ref[...] = jnp.zeros_like(w_ref)
    acc_ref[...] = jnp.zeros_like(acc_ref)

    def two_blocks(m, pending):
        pending = query_block(2 * m, 0, pending)
        return query_block(2 * m + 1, 1, pending)

    zero = jnp.zeros((), jnp.int32)
    block, last, slot = lax.fori_loop(0, n_blocks // 2, two_blocks, (zero, zero, zero))
    last_weights_times_v(last, slot, 1)
    write_output(block, 1)


def _attention(proj, bsz, seq):
    width = proj.shape[-1]
    proj3 = proj.reshape(bsz, seq, width)
    blk = 256
    pairs = SB_WIDTH // LANES

    def column_block(k):
        return pl.BlockSpec((None, seq, SB_WIDTH), lambda b: (b, 0, k))

    return pl.pallas_call(
        functools.partial(_attn_kernel, blk=blk, pairs=pairs),
        grid=(bsz,),
        in_specs=[column_block(0), column_block(1), column_block(2)],
        out_specs=pl.BlockSpec((None, seq, SB_WIDTH), lambda b: (b, 0, 0)),
        out_shape=jax.ShapeDtypeStruct((bsz, seq, SB_WIDTH), BF16),
        scratch_shapes=[pltpu.VMEM((pairs, 2 * blk, LANES), BF16),
                        pltpu.VMEM((pairs, 2 * blk, 1), F32),
                        pltpu.VMEM((2, pairs, 2 * blk, LANES), F32),
                        pltpu.VMEM((2, pairs, 2 * blk, blk), F32),
                        pltpu.VMEM((2, pairs, 2 * blk, blk), BF16),
                        pltpu.SMEM((1,), jnp.int32)],
        compiler_params=_params(("parallel",)),
        name="sb_attention",
    )(proj3, proj3, proj3).reshape(bsz * seq, SB_WIDTH)


def _gelu(x):
    return 0.5 * x * (1.0 + jnp.tanh(0.7978845608028654 * (x + 0.044715 * (x * x * x))))


def _load_weights_as_bf16(pairs, stage_ref, sem_ref):
    def copy(k):
        return pltpu.make_async_copy(pairs[k][0], stage_ref.at[k % 2], sem_ref.at[k % 2])

    copy(0).start()
    for k in range(len(pairs)):
        if k + 1 < len(pairs):
            copy(k + 1).start()
        copy(k).wait()
        pairs[k][1][...] = stage_ref[k % 2].astype(BF16)


def _mix_mlp_kernel(a_ref, bg_ref, cg_ref, hc_ref, us_ref, vs_ref, cgh_ref, hch_ref,
                    x_ref, gm_ref, cw_ref, cb_ref, ng_ref, sw_ref, sbias_ref, wo_hbm,
                    g_ref, sc_ref, sh_ref, gf_ref, w1_hbm, w2_hbm, fg_ref,
                    o_ref, uext_ref, wo_ref, w1_ref, w2_ref, stage_ref, sem_ref, *,
                    layer, tm, per_seq, halo, tf, final_norm):
    i = pl.program_id(0)

    @pl.when(i == 0)
    def _():
        rows, cols = stage_ref.shape[1:]
        d, d_ff = w1_ref.shape
        blocks = []
        for r in range(0, d, rows):
            for c in range(0, d_ff, cols):
                blocks.append((w1_hbm.at[layer, r:r + rows, c:c + cols],
                               w1_ref.at[r:r + rows, c:c + cols]))
        for r in range(0, d_ff, rows):
            blocks.append((w2_hbm.at[layer, r:r + rows, :], w2_ref.at[r:r + rows, :]))
        for r in range(0, d, rows):
            blocks.append((wo_hbm.at[layer, r:r + rows, :], wo_ref.at[r:r + rows, :]))
        _load_weights_as_bf16(blocks, stage_ref, sem_ref)

    u = cg_ref[...].astype(F32) * hc_ref[...].astype(F32)
    u_halo = cgh_ref[...].astype(F32) * hch_ref[...].astype(F32)
    u_halo = jnp.where(i % per_seq == 0, 0.0, u_halo)
    uext_ref[0:halo, :] = u_halo
    uext_ref[halo:halo + tm, :] = u
    u_m1 = uext_ref[halo - 1:halo - 1 + tm, :]
    u_m2 = uext_ref[halo - 2:halo - 2 + tm, :]
    cw = cw_ref[...]
    y = cw[0:1] * u_m2 + cw[1:2] * u_m1 + cw[2:3] * u + cb_ref[...]
    c_out = bg_ref[...].astype(F32) * y

    ug = _gelu(us_ref[...].astype(F32))
    vg = _gelu(vs_ref[...].astype(F32))
    vn = vg * lax.rsqrt(jnp.mean(vg * vg, axis=-1, keepdims=True) + EPS) * ng_ref[...]
    t_chunk = lax.broadcasted_iota(jnp.int32, (SG_CHUNK, SG_CHUNK), 0) // CHUNK
    s_chunk = lax.broadcasted_iota(jnp.int32, (SG_CHUNK, SG_CHUNK), 1) // CHUNK
    chunk_causal = t_chunk >= s_chunk
    w_cat = jnp.concatenate(
        [jnp.where(chunk_causal, sw_ref[h], 0.0) for h in range(SG_HEADS)], axis=1).astype(BF16)
    lane_head = lax.broadcasted_iota(jnp.int32, (SG_CHUNK, SG_WIDTH), 1) // HEAD_DIM
    sbias = sbias_ref[...]
    mixed = []
    for n in range(tm // SG_CHUNK):
        v_win = vn[n * SG_CHUNK:(n + 1) * SG_CHUNK]
        v_stack = jnp.concatenate(
            [jnp.where(lane_head == h, v_win, 0.0) for h in range(SG_HEADS)], axis=0).astype(BF16)
        mixed.append(jnp.dot(w_cat, v_stack, preferred_element_type=F32) + sbias)
    s_out = ug * jnp.concatenate(mixed, axis=0)

    cat = jnp.concatenate([a_ref[...], c_out.astype(BF16), s_out.astype(BF16)], axis=1)
    mix = jnp.dot(cat, wo_ref[...], preferred_element_type=F32)
    x = x_ref[...] + gm_ref[...] * mix

    h = _modulated_norm(x, g_ref[...], sc_ref[...], sh_ref[...]).astype(BF16)
    acc = None
    for lo in range(0, w1_ref.shape[-1], tf):
        t = jnp.dot(h, w1_ref[:, lo:lo + tf], preferred_element_type=F32)
        t = jnp.square(jnp.maximum(t, 0.0)).astype(BF16)
        part = jnp.dot(t, w2_ref[lo:lo + tf, :], preferred_element_type=F32)
        acc = part if acc is None else acc + part
    y = x + gf_ref[...] * acc
    if final_norm:
        y = y * lax.rsqrt(jnp.mean(y * y, axis=-1, keepdims=True) + EPS) * fg_ref[...]
    o_ref[...] = y


def _mix_mlp(a_out, proj, x, mod, conv_w, conv_b, gmlp_norm_g, spatial_w, sbias, w_out,
             norm_g, w1, w2, final_g, layer, seq, final_norm):
    n, d = x.shape
    d_ff = w1.shape[-1]
    bsz = n // seq
    tm = 512
    tf = 512
    per_seq = seq // tm
    halo = BF16_SUBLANES
    col0 = 3 * SB_WIDTH // SC_WIDTH
    resident = pl.Buffered(1)

    def col(k):
        return pl.BlockSpec((tm, SC_WIDTH), lambda i: (i, col0 + k))

    def halo_col(k):
        return pl.BlockSpec((halo, SC_WIDTH),
                            lambda i: (jnp.maximum(i * (tm // halo) - 1, 0), col0 + k))

    def mod_row(k):
        return pl.BlockSpec((None, 1, d),
                            lambda i: ((layer * bsz + i // per_seq) * N_MOD + k, 0, 0))

    def layer_block(*shape):
        zeros = (0,) * len(shape)
        return pl.BlockSpec((None,) + shape, lambda i: (layer,) + zeros, pipeline_mode=resident)

    in_hbm = pl.BlockSpec(memory_space=pl.ANY)
    stage_rows = 512

    return pl.pallas_call(
        functools.partial(_mix_mlp_kernel, layer=layer, tm=tm, per_seq=per_seq, halo=halo,
                          tf=tf, final_norm=final_norm),
        grid=(n // tm,),
        in_specs=[
            pl.BlockSpec((tm, SB_WIDTH), lambda i: (i, 0)),
            col(0), col(1), col(2), col(3), col(4),
            halo_col(1), halo_col(2),
            pl.BlockSpec((tm, d), lambda i: (i, 0)),
            mod_row(2),
            layer_block(3, SC_WIDTH),
            layer_block(1, SC_WIDTH),
            layer_block(1, SG_WIDTH),
            layer_block(SG_HEADS, SG_CHUNK, SG_CHUNK),
            layer_block(SG_CHUNK, SG_WIDTH),
            in_hbm,
            layer_block(1, d),
            mod_row(4), mod_row(3), mod_row(5),
            in_hbm,
            in_hbm,
            pl.BlockSpec((1, d), lambda i: (0, 0)),
        ],
        out_specs=pl.BlockSpec((tm, d), lambda i: (i, 0)),
        out_shape=jax.ShapeDtypeStruct((n, d), F32),
        scratch_shapes=[pltpu.VMEM((halo + tm, SC_WIDTH), F32),
                        pltpu.VMEM((d, d), BF16),
                        pltpu.VMEM((d, d_ff), BF16),
                        pltpu.VMEM((d_ff, d), BF16),
                        pltpu.VMEM((2, stage_rows, d), F32),
                        pltpu.SemaphoreType.DMA((2,))],
        compiler_params=_params(("arbitrary",)),
        name="mix_mlp",
    )(a_out, proj, proj, proj, proj, proj, proj, proj, x, mod,
      conv_w, conv_b, gmlp_norm_g, spatial_w, sbias, w_out,
      norm_g, mod, mod, mod, w1, w2, final_g)


def kernel(x, c, ada_w, ada_b, norm_mix_g, norm_mlp_g, w_in, conv_w, conv_b, gmlp_norm_g,
           spatial_w, spatial_b, w_out, mlp_w1, mlp_w2, final_norm_g):
    bsz, seq, d = x.shape
    depth = ada_w.shape[0]
    xf = x.reshape(bsz * seq, d)

    mod = _adaln(c, ada_w, ada_b).reshape(depth * bsz * N_MOD, 1, d)
    w_in_b = w_in.astype(BF16)
    norm_mix = norm_mix_g.reshape(depth, 1, d)
    norm_mlp = norm_mlp_g.reshape(depth, 1, d)
    conv_b3 = conv_b.reshape(depth, 1, SC_WIDTH)
    gmlp_g3 = gmlp_norm_g.reshape(depth, 1, SG_WIDTH)
    sbias = jnp.repeat(jnp.swapaxes(spatial_b, 1, 2), HEAD_DIM, axis=2)
    final_g = final_norm_g.reshape(1, d)

    for layer in range(depth):
        proj = _norm_proj(xf, norm_mix, mod, w_in_b, layer, seq)
        a_out = _attention(proj, bsz, seq)
        xf = _mix_mlp(a_out, proj, xf, mod, conv_w, conv_b3, gmlp_g3, spatial_w, sbias,
                      w_out, norm_mlp, mlp_w1, mlp_w2, final_g, layer, seq,
                      final_norm=(layer == depth - 1))
    return xf.reshape(bsz, seq, d)
```

```python
import functools

import jax
import jax.numpy as jnp
from jax import lax
from jax.experimental import pallas as pl
from jax.experimental.pallas import tpu as pltpu

F32 = jnp.float32
BF16 = jnp.bfloat16

HEAD_DIM = 64
SB_WIDTH = 512
SC_WIDTH = 256
SG_WIDTH = 256
SG_HEADS = 4
SG_CHUNK = 128
CHUNK = 64
N_MOD = 6
EPS = 1e-6
LANES = 128
BF16_SUBLANES = 16
LOG2E = 1.4426950408889634

VMEM_LIMIT = 48 * 1024 * 1024


def _params(sem):
    return pltpu.CompilerParams(dimension_semantics=sem, vmem_limit_bytes=VMEM_LIMIT)


def _adaln_kernel(c_ref, w_ref, b_ref, o_ref):
    c = c_ref[...]
    c_act = (c * jax.nn.sigmoid(c)).astype(BF16)
    o_ref[...] = jnp.dot(c_act, w_ref[...].astype(BF16), preferred_element_type=F32) + b_ref[...]


def _adaln(c, ada_w, ada_b):
    depth, d, width = ada_w.shape
    bsz = c.shape[0]
    tn = 1536
    return pl.pallas_call(
        _adaln_kernel,
        grid=(depth, width // tn),
        in_specs=[
            pl.BlockSpec((bsz, d), lambda l, j: (0, 0)),
            pl.BlockSpec((None, d, tn), lambda l, j: (l, 0, j)),
            pl.BlockSpec((None, 1, tn), lambda l, j: (l, 0, j)),
        ],
        out_specs=pl.BlockSpec((None, bsz, tn), lambda l, j: (l, 0, j)),
        out_shape=jax.ShapeDtypeStruct((depth, bsz, width), F32),
        compiler_params=_params(("parallel", "parallel")),
        name="adaln",
    )(c, ada_w, ada_b.reshape(depth, 1, width))


def _modulated_norm(x, g, sc, sh):
    r = lax.rsqrt(jnp.mean(x * x, axis=-1, keepdims=True) + EPS)
    return x * r * (g * (1.0 + sc)) + sh


def _norm_proj_kernel(x_ref, g_ref, sc_ref, sh_ref, w_ref, o_ref, *, sub, col_chunks):
    hs = [_modulated_norm(x_ref[r:r + sub, :], g_ref[...], sc_ref[...], sh_ref[...]).astype(BF16)
          for r in range(0, x_ref.shape[0], sub)]
    for k, h in enumerate(hs):
        for lo, hi in col_chunks:
            o_ref[k * sub:(k + 1) * sub, lo:hi] = jnp.dot(
                h, w_ref[:, lo:hi], preferred_element_type=F32).astype(BF16)


def _norm_proj(x, norm_g, mod, w_in, layer, seq):
    n, d = x.shape
    width = w_in.shape[-1]
    bsz = n // seq
    tm = 1024
    per_seq = seq // tm
    tn = 512
    col_chunks = tuple((lo, min(lo + tn, width)) for lo in range(0, width, tn))

    def mod_idx(k):
        return lambda i: ((layer * bsz + i // per_seq) * N_MOD + k, 0, 0)

    return pl.pallas_call(
        functools.partial(_norm_proj_kernel, sub=256, col_chunks=col_chunks),
        grid=(n // tm,),
        in_specs=[
            pl.BlockSpec((tm, d), lambda i: (i, 0)),
            pl.BlockSpec((None, 1, d), lambda i: (layer, 0, 0)),
            pl.BlockSpec((None, 1, d), mod_idx(1)),
            pl.BlockSpec((None, 1, d), mod_idx(0)),
            pl.BlockSpec((None, d, width), lambda i: (layer, 0, 0)),
        ],
        out_specs=pl.BlockSpec((tm, width), lambda i: (i, 0)),
        out_shape=jax.ShapeDtypeStruct((n, width), BF16),
        compiler_params=_params(("parallel",)),
        name="norm_proj",
    )(x, norm_g, mod, mod, w_in)


ZERO_WEIGHT_LOG2 = -140.0


def _attn_kernel(q_ref, k_ref, v_ref, o_ref, qq_ref, carry_ref, acc_ref, nz_ref, w_ref,
                 more_ref, *, blk, pairs):
    win = blk
    rows = 2 * blk
    n_blocks = q_ref.shape[0] // blk
    head0 = lax.broadcasted_iota(jnp.int32, (blk, LANES), 1) < HEAD_DIM
    qscale = -(HEAD_DIM ** -0.5) * LOG2E
    k_max = jnp.max(jnp.max(jnp.abs(k_ref[...]), axis=0, keepdims=True).astype(F32))

    r = lax.broadcasted_iota(jnp.int32, (win, win), 0)
    c = lax.broadcasted_iota(jnp.int32, (win, win), 1)
    cum = jnp.where(r >= c, 1.0, 0.0).astype(BF16)

    t_row = lax.broadcasted_iota(jnp.int32, (rows, win), 0) & (blk - 1)
    s_col = lax.broadcasted_iota(jnp.int32, (rows, win), 1)
    causal = s_col < t_row
    ps = range(pairs)

    def key_start(w):
        return pl.multiple_of(jnp.maximum(w, 0) * win, win)

    def last_weights_times_v(last, slot, buf):
        start = key_start(last)
        for p in ps:
            acc_ref[buf, p] += jnp.dot(
                w_ref[slot, p], v_ref[pl.ds(start, win), p * LANES:(p + 1) * LANES],
                preferred_element_type=F32)

    def write_output(block, buf):
        out_rows = pl.ds(pl.multiple_of(block * blk, blk), blk)
        for p in ps:
            acc = acc_ref[buf, p]
            o_ref[out_rows, p * LANES:(p + 1) * LANES] = jnp.where(
                head0, acc[:blk], acc[blk:]).astype(BF16)

    def query_block(first, buf, pending):
        q_rows = pl.ds(pl.multiple_of(first * blk, blk), blk)
        q_l1 = jnp.zeros((), F32)
        for p in ps:
            q = q_ref[q_rows, p * LANES:(p + 1) * LANES].astype(F32) * qscale
            qq = jnp.concatenate([jnp.where(head0, q, 0.0),
                                  jnp.where(head0, 0.0, q)], axis=0).astype(BF16)
            qq_ref[p] = qq
            q_l1 = jnp.maximum(q_l1, jnp.max(jnp.sum(jnp.abs(qq.astype(F32)), axis=1)))
        stop_below = ZERO_WEIGHT_LOG2 - 2.0 ** -8 * (1.01 * q_l1 * k_max + 2.0)

        def scores(w, slot, p):
            start = key_start(w)
            nz_ref[slot, p] = lax.dot_general(
                qq_ref[p], k_ref[pl.ds(start, win), p * LANES:(p + 1) * LANES],
                (((1,), (1,)), ((), ())), preferred_element_type=F32)

        def weights_times_v(w, slot, p):
            start = key_start(w)
            acc_ref[buf, p] += jnp.dot(w_ref[slot, p],
                                       v_ref[pl.ds(start, win), p * LANES:(p + 1) * LANES],
                                       preferred_element_type=F32)

        def step(w, slot, diagonal=False):
            incl, carries, top = [], [], None
            for p in ps:
                nz = nz_ref[slot, p]
                sp = jnp.log(1.0 + jnp.exp2(-jnp.abs(nz))) * LOG2E
                log_stay = jnp.minimum(nz, 0.0) - sp
                if diagonal:
                    log_stay = jnp.where(causal, log_stay, 0.0)
                incl.append(jnp.dot(log_stay.astype(BF16), cum, preferred_element_type=F32))
                if not diagonal:
                    weights_times_v(w + 1, 1 - slot, p)
                scores(w - 1, 1 - slot, p)
                total = incl[p][:, 0:1]
                if diagonal:
                    carries.append(None)
                else:
                    carries.append(carry_ref[p])
                    total = carries[p] + total
                carry_ref[p] = total
                top = total if top is None else jnp.maximum(top, total)
            more_ref[0] = (jnp.max(top) > stop_below).astype(jnp.int32)
            for p in ps:
                if diagonal:
                    wt = jnp.where(causal, jnp.exp2(incl[p] - nz_ref[slot, p]), 0.0)
                else:
                    wt = jnp.exp2(incl[p] + carries[p] - nz_ref[slot, p])
                w_ref[slot, p] = wt.astype(BF16)

        def unfinished():
            return more_ref[0] == 1

        prev_block, prev_last, prev_slot = pending
        acc_ref[buf] = jnp.zeros(acc_ref.shape[1:], F32)
        for p in ps:
            scores(first, 0, p)
        last_weights_times_v(prev_last, prev_slot, 1 - buf)
        step(first, 0, diagonal=True)
        write_output(prev_block, 1 - buf)

        def body(state):
            w, _, _ = state
            step(w, 1)
            go_second = unfinished() & (w >= 1)

            @pl.when(go_second)
            def _():
                step(w - 1, 0)

            go_on = go_second & (w >= 2) & unfinished()
            return w - 2, go_on, jnp.where(go_second, w - 1, w)

        _, _, last = lax.while_loop(lambda state: state[1], body,
                                    (first - 1, (first >= 1) & unfinished(), first))
        return first, last, (first - last) & 1

    w_ref[...] = jnp.zeros_like(w_ref)
    acc_ref[...] = jnp.zeros_like(acc_ref)

    def two_blocks(m, pending):
        pending = query_block(2 * m, 0, pending)
        return query_block(2 * m + 1, 1, pending)

    zero = jnp.zeros((), jnp.int32)
    block, last, slot = lax.fori_loop(0, n_blocks // 2, two_blocks, (zero, zero, zero))
    last_weights_times_v(last, slot, 1)
    write_output(block, 1)


def _attention(proj, bsz, seq):
    width = proj.shape[-1]
    proj3 = proj.reshape(bsz, seq, width)
    blk = 256
    pairs = SB_WIDTH // LANES

    def column_block(k):
        return pl.BlockSpec((None, seq, SB_WIDTH), lambda b: (b, 0, k))

    return pl.pallas_call(
        functools.partial(_attn_kernel, blk=blk, pairs=pairs),
        grid=(bsz,),
        in_specs=[column_block(0), column_block(1), column_block(2)],
        out_specs=pl.BlockSpec((None, seq, SB_WIDTH), lambda b: (b, 0, 0)),
        out_shape=jax.ShapeDtypeStruct((bsz, seq, SB_WIDTH), BF16),
        scratch_shapes=[pltpu.VMEM((pairs, 2 * blk, LANES), BF16),
                        pltpu.VMEM((pairs, 2 * blk, 1), F32),
                        pltpu.VMEM((2, pairs, 2 * blk, LANES), F32),
                        pltpu.VMEM((2, pairs, 2 * blk, blk), F32),
                        pltpu.VMEM((2, pairs, 2 * blk, blk), BF16),
                        pltpu.SMEM((1,), jnp.int32)],
        compiler_params=_params(("parallel",)),
        name="sb_attention",
    )(proj3, proj3, proj3).reshape(bsz * seq, SB_WIDTH)


def _gelu(x):
    return 0.5 * x * (1.0 + jnp.tanh(0.7978845608028654 * (x + 0.044715 * (x * x * x))))


def _mix_mlp_kernel(a_ref, bg_ref, cg_ref, hc_ref, us_ref, vs_ref, cgh_ref, hch_ref,
                    x_ref, gm_ref, cw_ref, cb_ref, ng_ref, sw_ref, sbias_ref, wo_hbm,
                    g_ref, sc_ref, sh_ref, gf_ref, w1_hbm, w2_hbm, fg_ref,
                    o_ref, uext_ref, wo_ref, w1_ref, w2_ref, stage1_ref, stage2_ref, sem_ref, *,
                    layer, tm, per_seq, halo, tf, final_norm):
    i = pl.program_id(0)
    d, d_ff = w1_ref.shape
    n_chunks = d_ff // tf

    def w1_copy(f):
        return pltpu.make_async_copy(w1_hbm.at[layer, :, f * tf:(f + 1) * tf],
                                     stage1_ref.at[f % 2], sem_ref.at[0, f % 2])

    def w2_copy(f):
        return pltpu.make_async_copy(w2_hbm.at[layer, f * tf:(f + 1) * tf, :],
                                     stage2_ref.at[f % 2], sem_ref.at[1, f % 2])

    def wo_copy(r):
        return pltpu.make_async_copy(wo_hbm.at[layer, r * tf:(r + 1) * tf, :],
                                     stage2_ref.at[r], sem_ref.at[1, r])

    def tile(load_weights):
        if load_weights:
            for r in range(2):
                wo_copy(r).start()
                w1_copy(r).start()

        u = cg_ref[...].astype(F32) * hc_ref[...].astype(F32)
        u_halo = cgh_ref[...].astype(F32) * hch_ref[...].astype(F32)
        u_halo = jnp.where(i % per_seq == 0, 0.0, u_halo)
        uext_ref[0:halo, :] = u_halo
        uext_ref[halo:halo + tm, :] = u
        u_m1 = uext_ref[halo - 1:halo - 1 + tm, :]
        u_m2 = uext_ref[halo - 2:halo - 2 + tm, :]
        cw = cw_ref[...]
        y = cw[0:1] * u_m2 + cw[1:2] * u_m1 + cw[2:3] * u + cb_ref[...]
        c_out = bg_ref[...].astype(F32) * y

        ug = _gelu(us_ref[...].astype(F32))
        vg = _gelu(vs_ref[...].astype(F32))
        vn = vg * lax.rsqrt(jnp.mean(vg * vg, axis=-1, keepdims=True) + EPS) * ng_ref[...]
        t_chunk = lax.broadcasted_iota(jnp.int32, (SG_CHUNK, SG_CHUNK), 0) // CHUNK
        s_chunk = lax.broadcasted_iota(jnp.int32, (SG_CHUNK, SG_CHUNK), 1) // CHUNK
        chunk_causal = t_chunk >= s_chunk
        w_cat = jnp.concatenate(
            [jnp.where(chunk_causal, sw_ref[h], 0.0) for h in range(SG_HEADS)],
            axis=1).astype(BF16)
        lane_head = lax.broadcasted_iota(jnp.int32, (SG_CHUNK, SG_WIDTH), 1) // HEAD_DIM
        sbias = sbias_ref[...]
        mixed = []
        for n in range(tm // SG_CHUNK):
            v_win = vn[n * SG_CHUNK:(n + 1) * SG_CHUNK]
            v_stack = jnp.concatenate(
                [jnp.where(lane_head == h, v_win, 0.0) for h in range(SG_HEADS)],
                axis=0).astype(BF16)
            mixed.append(jnp.dot(w_cat, v_stack, preferred_element_type=F32) + sbias)
        s_out = ug * jnp.concatenate(mixed, axis=0)
        cat = jnp.concatenate([a_ref[...], c_out.astype(BF16), s_out.astype(BF16)], axis=1)

        if load_weights:
            for r in range(2):
                wo_copy(r).wait()
                wo_ref[r * tf:(r + 1) * tf, :] = stage2_ref[r].astype(BF16)
                w2_copy(r).start()
        mix = jnp.dot(cat, wo_ref[...], preferred_element_type=F32)
        x = x_ref[...] + gm_ref[...] * mix

        h = _modulated_norm(x, g_ref[...], sc_ref[...], sh_ref[...]).astype(BF16)
        acc = None
        for f in range(n_chunks):
            cols = slice(f * tf, (f + 1) * tf)
            if load_weights:
                w1_copy(f).wait()
                w1_ref[:, cols] = stage1_ref[f % 2].astype(BF16)
                w2_copy(f).wait()
                w2_ref[cols, :] = stage2_ref[f % 2].astype(BF16)
                if f + 2 < n_chunks:
                    w1_copy(f + 2).start()
                    w2_copy(f + 2).start()
            t = jnp.dot(h, w1_ref[:, cols], preferred_element_type=F32)
            t = jnp.square(jnp.maximum(t, 0.0)).astype(BF16)
            part = jnp.dot(t, w2_ref[cols, :], preferred_element_type=F32)
            acc = part if acc is None else acc + part
        y = x + gf_ref[...] * acc
        if final_norm:
            y = y * lax.rsqrt(jnp.mean(y * y, axis=-1, keepdims=True) + EPS) * fg_ref[...]
        o_ref[...] = y

    @pl.when(i == 0)
    def _():
        tile(load_weights=True)

    @pl.when(i > 0)
    def _():
        tile(load_weights=False)


def _mix_mlp(a_out, proj, x, mod, conv_w, conv_b, gmlp_norm_g, spatial_w, sbias, w_out,
             norm_g, w1, w2, final_g, layer, seq, final_norm):
    n, d = x.shape
    d_ff = w1.shape[-1]
    bsz = n // seq
    tm = 512
    tf = 512
    per_seq = seq // tm
    halo = BF16_SUBLANES
    col0 = 3 * SB_WIDTH // SC_WIDTH
    resident = pl.Buffered(1)

    def col(k):
        return pl.BlockSpec((tm, SC_WIDTH), lambda i: (i, col0 + k))

    def halo_col(k):
        return pl.BlockSpec((halo, SC_WIDTH),
                            lambda i: (jnp.maximum(i * (tm // halo) - 1, 0), col0 + k))

    def mod_row(k):
        return pl.BlockSpec((None, 1, d),
                            lambda i: ((layer * bsz + i // per_seq) * N_MOD + k, 0, 0))

    def layer_block(*shape):
        zeros = (0,) * len(shape)
        return pl.BlockSpec((None,) + shape, lambda i: (layer,) + zeros, pipeline_mode=resident)

    in_hbm = pl.BlockSpec(memory_space=pl.ANY)
    assert d == 2 * tf and d_ff % tf == 0

    return pl.pallas_call(
        functools.partial(_mix_mlp_kernel, layer=layer, tm=tm, per_seq=per_seq, halo=halo,
                          tf=tf, final_norm=final_norm),
        grid=(n // tm,),
        in_specs=[
            pl.BlockSpec((tm, SB_WIDTH), lambda i: (i, 0)),
            col(0), col(1), col(2), col(3), col(4),
            halo_col(1), halo_col(2),
            pl.BlockSpec((tm, d), lambda i: (i, 0)),
            mod_row(2),
            layer_block(3, SC_WIDTH),
            layer_block(1, SC_WIDTH),
            layer_block(1, SG_WIDTH),
            layer_block(SG_HEADS, SG_CHUNK, SG_CHUNK),
            layer_block(SG_CHUNK, SG_WIDTH),
            in_hbm,
            layer_block(1, d),
            mod_row(4), mod_row(3), mod_row(5),
            in_hbm,
            in_hbm,
            pl.BlockSpec((1, d), lambda i: (0, 0)),
        ],
        out_specs=pl.BlockSpec((tm, d), lambda i: (i, 0)),
        out_shape=jax.ShapeDtypeStruct((n, d), F32),
        scratch_shapes=[pltpu.VMEM((halo + tm, SC_WIDTH), F32),
                        pltpu.VMEM((d, d), BF16),
                        pltpu.VMEM((d, d_ff), BF16),
                        pltpu.VMEM((d_ff, d), BF16),
                        pltpu.VMEM((2, d, tf), F32),
                        pltpu.VMEM((2, tf, d), F32),
                        pltpu.SemaphoreType.DMA((2, 2))],
        compiler_params=_params(("arbitrary",)),
        name="mix_mlp",
    )(a_out, proj, proj, proj, proj, proj, proj, proj, x, mod,
      conv_w, conv_b, gmlp_norm_g, spatial_w, sbias, w_out,
      norm_g, mod, mod, mod, w1, w2, final_g)


def kernel(x, c, ada_w, ada_b, norm_mix_g, norm_mlp_g, w_in, conv_w, conv_b, gmlp_norm_g,
           spatial_w, spatial_b, w_out, mlp_w1, mlp_w2, final_norm_g):
    bsz, seq, d = x.shape
    depth = ada_w.shape[0]
    xf = x.reshape(bsz * seq, d)

    mod = _adaln(c, ada_w, ada_b).reshape(depth * bsz * N_MOD, 1, d)
    w_in_b = w_in.astype(BF16)
    norm_mix = norm_mix_g.reshape(depth, 1, d)
    norm_mlp = norm_mlp_g.reshape(depth, 1, d)
    conv_b3 = conv_b.reshape(depth, 1, SC_WIDTH)
    gmlp_g3 = gmlp_norm_g.reshape(depth, 1, SG_WIDTH)
    sbias = jnp.repeat(jnp.swapaxes(spatial_b, 1, 2), HEAD_DIM, axis=2)
    final_g = final_norm_g.reshape(1, d)

    for layer in range(depth):
        proj = _norm_proj(xf, norm_mix, mod, w_in_b, layer, seq)
        a_out = _attention(proj, bsz, seq)
        xf = _mix_mlp(a_out, proj, xf, mod, conv_w, conv_b3, gmlp_g3, spatial_w, sbias,
                      w_out, norm_mlp, mlp_w1, mlp_w2, final_g, layer, seq,
                      final_norm=(layer == depth - 1))
    return xf.reshape(bsz, seq, d)
```

```python
import functools

import jax
import jax.numpy as jnp
from jax import lax
from jax.experimental import pallas as pl
from jax.experimental.pallas import tpu as pltpu

F32 = jnp.float32
BF16 = jnp.bfloat16

HEAD_DIM = 64
SB_WIDTH = 512
SC_WIDTH = 256
SG_WIDTH = 256
SG_HEADS = 4
SG_CHUNK = 128
CHUNK = 64
N_MOD = 6
EPS = 1e-6
LANES = 128
BF16_SUBLANES = 16
LOG2E = 1.4426950408889634

VMEM_LIMIT = 48 * 1024 * 1024


def _params(sem):
    return pltpu.CompilerParams(dimension_semantics=sem, vmem_limit_bytes=VMEM_LIMIT)


def _adaln_kernel(c_ref, w_ref, b_ref, o_ref):
    c = c_ref[...]
    c_act = (c * jax.nn.sigmoid(c)).astype(BF16)
    o_ref[...] = jnp.dot(c_act, w_ref[...].astype(BF16), preferred_element_type=F32) + b_ref[...]


def _adaln(c, ada_w, ada_b):
    depth, d, width = ada_w.shape
    bsz = c.shape[0]
    tn = 1536
    return pl.pallas_call(
        _adaln_kernel,
        grid=(depth, width // tn),
        in_specs=[
            pl.BlockSpec((bsz, d), lambda l, j: (0, 0)),
            pl.BlockSpec((None, d, tn), lambda l, j: (l, 0, j)),
            pl.BlockSpec((None, 1, tn), lambda l, j: (l, 0, j)),
        ],
        out_specs=pl.BlockSpec((None, bsz, tn), lambda l, j: (l, 0, j)),
        out_shape=jax.ShapeDtypeStruct((depth, bsz, width), F32),
        compiler_params=_params(("parallel", "parallel")),
        name="adaln",
    )(c, ada_w, ada_b.reshape(depth, 1, width))


def _modulated_norm(x, g, sc, sh):
    r = lax.rsqrt(jnp.mean(x * x, axis=-1, keepdims=True) + EPS)
    return x * r * (g * (1.0 + sc)) + sh


def _norm_proj_kernel(x_ref, g_ref, sc_ref, sh_ref, w_hbm, o_ref, w_ref, stage_ref, sem_ref, *,
                      layer, sub, col_chunks):
    i = pl.program_id(0)

    def w_copy(c):
        lo, hi = col_chunks[c]
        return pltpu.make_async_copy(w_hbm.at[layer, :, lo:hi],
                                     stage_ref.at[c % 2, :, 0:hi - lo], sem_ref.at[c % 2])

    def tile(load_weights):
        if load_weights:
            w_copy(0).start()
            w_copy(1).start()
        hs = [_modulated_norm(x_ref[r:r + sub, :], g_ref[...], sc_ref[...],
                              sh_ref[...]).astype(BF16)
              for r in range(0, x_ref.shape[0], sub)]
        for k, h in enumerate(hs):
            for c, (lo, hi) in enumerate(col_chunks):
                if load_weights and k == 0:
                    w_copy(c).wait()
                    w_ref[:, lo:hi] = stage_ref[c % 2, :, 0:hi - lo].astype(BF16)
                    if c + 2 < len(col_chunks):
                        w_copy(c + 2).start()
                o_ref[k * sub:(k + 1) * sub, lo:hi] = jnp.dot(
                    h, w_ref[:, lo:hi], preferred_element_type=F32).astype(BF16)

    @pl.when(i == 0)
    def _():
        tile(load_weights=True)

    @pl.when(i > 0)
    def _():
        tile(load_weights=False)


def _norm_proj(x, norm_g, mod, w_in, layer, seq):
    n, d = x.shape
    width = w_in.shape[-1]
    bsz = n // seq
    tm = 1024
    per_seq = seq // tm
    tn = 512
    col_chunks = tuple((lo, min(lo + tn, width)) for lo in range(0, width, tn))

    def mod_idx(k):
        return lambda i: ((layer * bsz + i // per_seq) * N_MOD + k, 0, 0)

    return pl.pallas_call(
        functools.partial(_norm_proj_kernel, layer=layer, sub=256, col_chunks=col_chunks),
        grid=(n // tm,),
        in_specs=[
            pl.BlockSpec((tm, d), lambda i: (i, 0)),
            pl.BlockSpec((None, 1, d), lambda i: (layer, 0, 0)),
            pl.BlockSpec((None, 1, d), mod_idx(1)),
            pl.BlockSpec((None, 1, d), mod_idx(0)),
            pl.BlockSpec(memory_space=pl.ANY),
        ],
        out_specs=pl.BlockSpec((tm, width), lambda i: (i, 0)),
        out_shape=jax.ShapeDtypeStruct((n, width), BF16),
        scratch_shapes=[pltpu.VMEM((d, width), BF16),
                        pltpu.VMEM((2, d, tn), F32),
                        pltpu.SemaphoreType.DMA((2,))],
        compiler_params=_params(("arbitrary",)),
        name="norm_proj",
    )(x, norm_g, mod, mod, w_in)


ZERO_WEIGHT_LOG2 = -140.0


def _attn_kernel(q_ref, k_ref, v_ref, o_ref, qq_ref, carry_ref, acc_ref, nz_ref, w_ref,
                 more_ref, *, blk, pairs):
    win = blk
    rows = 2 * blk
    n_blocks = q_ref.shape[0] // blk
    head0 = lax.broadcasted_iota(jnp.int32, (blk, LANES), 1) < HEAD_DIM
    qscale = -(HEAD_DIM ** -0.5) * LOG2E
    k_max = jnp.max(jnp.max(jnp.abs(k_ref[...]), axis=0, keepdims=True).astype(F32))

    r = lax.broadcasted_iota(jnp.int32, (win, win), 0)
    c = lax.broadcasted_iota(jnp.int32, (win, win), 1)
    cum = jnp.where(r >= c, 1.0, 0.0).astype(BF16)

    t_row = lax.broadcasted_iota(jnp.int32, (rows, win), 0) & (blk - 1)
    s_col = lax.broadcasted_iota(jnp.int32, (rows, win), 1)
    causal = s_col < t_row
    ps = range(pairs)

    def key_start(w):
        return pl.multiple_of(jnp.maximum(w, 0) * win, win)

    def last_weights_times_v(last, slot, buf):
        start = key_start(last)
        for p in ps:
            acc_ref[buf, p] += jnp.dot(
                w_ref[slot, p], v_ref[pl.ds(start, win), p * LANES:(p + 1) * LANES],
                preferred_element_type=F32)

    def write_output(block, buf):
        out_rows = pl.ds(pl.multiple_of(block * blk, blk), blk)
        for p in ps:
            acc = acc_ref[buf, p]
            o_ref[out_rows, p * LANES:(p + 1) * LANES] = jnp.where(
                head0, acc[:blk], acc[blk:]).astype(BF16)

    def query_block(first, buf, pending):
        q_rows = pl.ds(pl.multiple_of(first * blk, blk), blk)
        q_l1 = jnp.zeros((), F32)
        for p in ps:
            q = q_ref[q_rows, p * LANES:(p + 1) * LANES].astype(F32) * qscale
            qq = jnp.concatenate([jnp.where(head0, q, 0.0),
                                  jnp.where(head0, 0.0, q)], axis=0).astype(BF16)
            qq_ref[p] = qq
            q_l1 = jnp.maximum(q_l1, jnp.max(jnp.sum(jnp.abs(qq.astype(F32)), axis=1)))
        stop_below = ZERO_WEIGHT_LOG2 - 2.0 ** -8 * (1.01 * q_l1 * k_max + 2.0)

        def scores(w, slot, p):
            start = key_start(w)
            nz_ref[slot, p] = lax.dot_general(
                qq_ref[p], k_ref[pl.ds(start, win), p * LANES:(p + 1) * LANES],
                (((1,), (1,)), ((), ())), preferred_element_type=F32)

        def weights_times_v(w, slot, p):
            start = key_start(w)
            acc_ref[buf, p] += jnp.dot(w_ref[slot, p],
                                       v_ref[pl.ds(start, win), p * LANES:(p + 1) * LANES],
                                       preferred_element_type=F32)

        def step(w, slot, diagonal=False):
            incl, carries, top = [], [], None
            for p in ps:
                nz = nz_ref[slot, p]
                sp = jnp.log(1.0 + jnp.exp2(-jnp.abs(nz))) * LOG2E
                log_stay = jnp.minimum(nz, 0.0) - sp
                if diagonal:
                    log_stay = jnp.where(causal, log_stay, 0.0)
                incl.append(jnp.dot(log_stay.astype(BF16), cum, preferred_element_type=F32))
                if not diagonal:
                    weights_times_v(w + 1, 1 - slot, p)
                scores(w - 1, 1 - slot, p)
                total = incl[p][:, 0:1]
                if diagonal:
                    carries.append(None)
                else:
                    carries.append(carry_ref[p])
                    total = carries[p] + total
                carry_ref[p] = total
                top = total if top is None else jnp.maximum(top, total)
            more_ref[0] = (jnp.max(top) > stop_below).astype(jnp.int32)
            for p in ps:
                if diagonal:
                    wt = jnp.where(causal, jnp.exp2(incl[p] - nz_ref[slot, p]), 0.0)
                else:
                    wt = jnp.exp2(incl[p] + carries[p] - nz_ref[slot, p])
                w_ref[slot, p] = wt.astype(BF16)

        def unfinished():
            return more_ref[0] == 1

        prev_block, prev_last, prev_slot = pending
        acc_ref[buf] = jnp.zeros(acc_ref.shape[1:], F32)
        for p in ps:
            scores(first, 0, p)
        last_weights_times_v(prev_last, prev_slot, 1 - buf)
        step(first, 0, diagonal=True)
        write_output(prev_block, 1 - buf)

        def body(state):
            w, _, _ = state
            step(w, 1)
            go_second = unfinished() & (w >= 1)

            @pl.when(go_second)
            def _():
                step(w - 1, 0)

            go_on = go_second & (w >= 2) & unfinished()
            return w - 2, go_on, jnp.where(go_second, w - 1, w)

        _, _, last = lax.while_loop(lambda state: state[1], body,
                                    (first - 1, (first >= 1) & unfinished(), first))
        return first, last, (first - last) & 1

    w_ref[...] = jnp.zeros_like(w_ref)
    acc_ref[...] = jnp.zeros_like(acc_ref)

    def two_blocks(m, pending):
        pending = query_block(2 * m, 0, pending)
        return query_block(2 * m + 1, 1, pending)

    zero = jnp.zeros((), jnp.int32)
    block, last, slot = lax.fori_loop(0, n_blocks // 2, two_blocks, (zero, zero, zero))
    last_weights_times_v(last, slot, 1)
    write_output(block, 1)


def _attention(proj, bsz, seq):
    width = proj.shape[-1]
    proj3 = proj.reshape(bsz, seq, width)
    blk = 256
    pairs = SB_WIDTH // LANES

    def column_block(k):
        return pl.BlockSpec((None, seq, SB_WIDTH), lambda b: (b, 0, k))

    return pl.pallas_call(
        functools.partial(_attn_kernel, blk=blk, pairs=pairs),
        grid=(bsz,),
        in_specs=[column_block(0), column_block(1), column_block(2)],
        out_specs=pl.BlockSpec((None, seq, SB_WIDTH), lambda b: (b, 0, 0)),
        out_shape=jax.ShapeDtypeStruct((bsz, seq, SB_WIDTH), BF16),
        scratch_shapes=[pltpu.VMEM((pairs, 2 * blk, LANES), BF16),
                        pltpu.VMEM((pairs, 2 * blk, 1), F32),
                        pltpu.VMEM((2, pairs, 2 * blk, LANES), F32),
                        pltpu.VMEM((2, pairs, 2 * blk, blk), F32),
                        pltpu.VMEM((2, pairs, 2 * blk, blk), BF16),
                        pltpu.SMEM((1,), jnp.int32)],
        compiler_params=_params(("parallel",)),
        name="sb_attention",
    )(proj3, proj3, proj3).reshape(bsz * seq, SB_WIDTH)


def _gelu(x):
    return 0.5 * x * (1.0 + jnp.tanh(0.7978845608028654 * (x + 0.044715 * (x * x * x))))


def _mix_mlp_kernel(a_ref, bg_ref, cg_ref, hc_ref, us_ref, vs_ref, cgh_ref, hch_ref,
                    x_ref, gm_ref, cw_ref, cb_ref, ng_ref, sw_ref, sbias_ref, wo_hbm,
                    g_ref, sc_ref, sh_ref, gf_ref, w1_hbm, w2_hbm, fg_ref,
                    o_ref, uext_ref, wo_ref, w1_ref, w2_ref, stage1_ref, stage2_ref, sem_ref, *,
                    layer, tm, per_seq, halo, tf, final_norm):
    i = pl.program_id(0)
    d, d_ff = w1_ref.shape
    n_chunks = d_ff // tf

    def w1_copy(f):
        return pltpu.make_async_copy(w1_hbm.at[layer, :, f * tf:(f + 1) * tf],
                                     stage1_ref.at[f % 2], sem_ref.at[0, f % 2])

    def w2_copy(f):
        return pltpu.make_async_copy(w2_hbm.at[layer, f * tf:(f + 1) * tf, :],
                                     stage2_ref.at[f % 2], sem_ref.at[1, f % 2])

    def wo_copy(r):
        return pltpu.make_async_copy(wo_hbm.at[layer, r * tf:(r + 1) * tf, :],
                                     stage2_ref.at[r], sem_ref.at[1, r])

    def tile(load_weights):
        if load_weights:
            for r in range(2):
                wo_copy(r).start()
                w1_copy(r).start()

        u = cg_ref[...].astype(F32) * hc_ref[...].astype(F32)
        u_halo = cgh_ref[...].astype(F32) * hch_ref[...].astype(F32)
        u_halo = jnp.where(i % per_seq == 0, 0.0, u_halo)
        uext_ref[0:halo, :] = u_halo
        uext_ref[halo:halo + tm, :] = u
        u_m1 = uext_ref[halo - 1:halo - 1 + tm, :]
        u_m2 = uext_ref[halo - 2:halo - 2 + tm, :]
        cw = cw_ref[...]
        y = cw[0:1] * u_m2 + cw[1:2] * u_m1 + cw[2:3] * u + cb_ref[...]
        c_out = bg_ref[...].astype(F32) * y

        ug = _gelu(us_ref[...].astype(F32))
        vg = _gelu(vs_ref[...].astype(F32))
        vn = vg * lax.rsqrt(jnp.mean(vg * vg, axis=-1, keepdims=True) + EPS) * ng_ref[...]
        t_chunk = lax.broadcasted_iota(jnp.int32, (SG_CHUNK, SG_CHUNK), 0) // CHUNK
        s_chunk = lax.broadcasted_iota(jnp.int32, (SG_CHUNK, SG_CHUNK), 1) // CHUNK
        chunk_causal = t_chunk >= s_chunk
        w_cat = jnp.concatenate(
            [jnp.where(chunk_causal, sw_ref[h], 0.0) for h in range(SG_HEADS)],
            axis=1).astype(BF16)
        lane_head = lax.broadcasted_iota(jnp.int32, (SG_CHUNK, SG_WIDTH), 1) // HEAD_DIM
        sbias = sbias_ref[...]
        mixed = []
        for n in range(tm // SG_CHUNK):
            v_win = vn[n * SG_CHUNK:(n + 1) * SG_CHUNK]
            v_stack = jnp.concatenate(
                [jnp.where(lane_head == h, v_win, 0.0) for h in range(SG_HEADS)],
                axis=0).astype(BF16)
            mixed.append(jnp.dot(w_cat, v_stack, preferred_element_type=F32) + sbias)
        s_out = ug * jnp.concatenate(mixed, axis=0)
        cat = jnp.concatenate([a_ref[...], c_out.astype(BF16), s_out.astype(BF16)], axis=1)

        if load_weights:
            for r in range(2):
                wo_copy(r).wait()
                wo_ref[r * tf:(r + 1) * tf, :] = stage2_ref[r].astype(BF16)
                w2_copy(r).start()
        mix = jnp.dot(cat, wo_ref[...], preferred_element_type=F32)
        x = x_ref[...] + gm_ref[...] * mix

        h = _modulated_norm(x, g_ref[...], sc_ref[...], sh_ref[...]).astype(BF16)
        acc = None
        for f in range(n_chunks):
            cols = slice(f * tf, (f + 1) * tf)
            if load_weights:
                w1_copy(f).wait()
                w1_ref[:, cols] = stage1_ref[f % 2].astype(BF16)
                w2_copy(f).wait()
                w2_ref[cols, :] = stage2_ref[f % 2].astype(BF16)
                if f + 2 < n_chunks:
                    w1_copy(f + 2).start()
                    w2_copy(f + 2).start()
            t = jnp.dot(h, w1_ref[:, cols], preferred_element_type=F32)
            t = jnp.square(jnp.maximum(t, 0.0)).astype(BF16)
            part = jnp.dot(t, w2_ref[cols, :], preferred_element_type=F32)
            acc = part if acc is None else acc + part
        y = x + gf_ref[...] * acc
        if final_norm:
            y = y * lax.rsqrt(jnp.mean(y * y, axis=-1, keepdims=True) + EPS) * fg_ref[...]
        o_ref[...] = y

    @pl.when(i == 0)
    def _():
        tile(load_weights=True)

    @pl.when(i > 0)
    def _():
        tile(load_weights=False)


def _mix_mlp(a_out, proj, x, mod, conv_w, conv_b, gmlp_norm_g, spatial_w, sbias, w_out,
             norm_g, w1, w2, final_g, layer, seq, final_norm):
    n, d = x.shape
    d_ff = w1.shape[-1]
    bsz = n // seq
    tm = 512
    tf = 512
    per_seq = seq // tm
    halo = BF16_SUBLANES
    col0 = 3 * SB_WIDTH // SC_WIDTH
    resident = pl.Buffered(1)

    def col(k):
        return pl.BlockSpec((tm, SC_WIDTH), lambda i: (i, col0 + k))

    def halo_col(k):
        return pl.BlockSpec((halo, SC_WIDTH),
                            lambda i: (jnp.maximum(i * (tm // halo) - 1, 0), col0 + k))

    def mod_row(k):
        return pl.BlockSpec((None, 1, d),
                            lambda i: ((layer * bsz + i // per_seq) * N_MOD + k, 0, 0))

    def layer_block(*shape):
        zeros = (0,) * len(shape)
        return pl.BlockSpec((None,) + shape, lambda i: (layer,) + zeros, pipeline_mode=resident)

    in_hbm = pl.BlockSpec(memory_space=pl.ANY)
    assert d == 2 * tf and d_ff % tf == 0

    return pl.pallas_call(
        functools.partial(_mix_mlp_kernel, layer=layer, tm=tm, per_seq=per_seq, halo=halo,
                          tf=tf, final_norm=final_norm),
        grid=(n // tm,),
        in_specs=[
            pl.BlockSpec((tm, SB_WIDTH), lambda i: (i, 0)),
            col(0), col(1), col(2), col(3), col(4),
            halo_col(1), halo_col(2),
            pl.BlockSpec((tm, d), lambda i: (i, 0)),
            mod_row(2),
            layer_block(3, SC_WIDTH),
            layer_block(1, SC_WIDTH),
            layer_block(1, SG_WIDTH),
            layer_block(SG_HEADS, SG_CHUNK, SG_CHUNK),
            layer_block(SG_CHUNK, SG_WIDTH),
            in_hbm,
            layer_block(1, d),
            mod_row(4), mod_row(3), mod_row(5),
            in_hbm,
            in_hbm,
            pl.BlockSpec((1, d), lambda i: (0, 0)),
        ],
        out_specs=pl.BlockSpec((tm, d), lambda i: (i, 0)),
        out_shape=jax.ShapeDtypeStruct((n, d), F32),
        scratch_shapes=[pltpu.VMEM((halo + tm, SC_WIDTH), F32),
                        pltpu.VMEM((d, d), BF16),
                        pltpu.VMEM((d, d_ff), BF16),
                        pltpu.VMEM((d_ff, d), BF16),
                        pltpu.VMEM((2, d, tf), F32),
                        pltpu.VMEM((2, tf, d), F32),
                        pltpu.SemaphoreType.DMA((2, 2))],
        compiler_params=_params(("arbitrary",)),
        name="mix_mlp",
    )(a_out, proj, proj, proj, proj, proj, proj, proj, x, mod,
      conv_w, conv_b, gmlp_norm_g, spatial_w, sbias, w_out,
      norm_g, mod, mod, mod, w1, w2, final_g)


def kernel(x, c, ada_w, ada_b, norm_mix_g, norm_mlp_g, w_in, conv_w, conv_b, gmlp_norm_g,
           spatial_w, spatial_b, w_out, mlp_w1, mlp_w2, final_norm_g):
    bsz, seq, d = x.shape
    depth = ada_w.shape[0]
    xf = x.reshape(bsz * seq, d)

    mod = _adaln(c, ada_w, ada_b).reshape(depth * bsz * N_MOD, 1, d)
    norm_mix = norm_mix_g.reshape(depth, 1, d)
    norm_mlp = norm_mlp_g.reshape(depth, 1, d)
    conv_b3 = conv_b.reshape(depth, 1, SC_WIDTH)
    gmlp_g3 = gmlp_norm_g.reshape(depth, 1, SG_WIDTH)
    sbias = jnp.repeat(jnp.swapaxes(spatial_b, 1, 2), HEAD_DIM, axis=2)
    final_g = final_norm_g.reshape(1, d)

    for layer in range(depth):
        proj = _norm_proj(xf, norm_mix, mod, w_in, layer, seq)
        a_out = _attention(proj, bsz, seq)
        xf = _mix_mlp(a_out, proj, xf, mod, conv_w, conv_b3, gmlp_g3, spatial_w, sbias,
                      w_out, norm_mlp, mlp_w1, mlp_w2, final_g, layer, seq,
                      final_norm=(layer == depth - 1))
    return xf.reshape(bsz, seq, d)
```

```python
import functools

import jax
import jax.numpy as jnp
from jax import lax
from jax.experimental import pallas as pl
from jax.experimental.pallas import tpu as pltpu

F32 = jnp.float32
BF16 = jnp.bfloat16

HEAD_DIM = 64
SB_WIDTH = 512
SC_WIDTH = 256
SG_WIDTH = 256
SG_HEADS = 4
SG_CHUNK = 128
CHUNK = 64
N_MOD = 6
EPS = 1e-6
LANES = 128
BF16_SUBLANES = 16
LOG2E = 1.4426950408889634

VMEM_LIMIT = 48 * 1024 * 1024


def _params(sem):
    return pltpu.CompilerParams(dimension_semantics=sem, vmem_limit_bytes=VMEM_LIMIT)


def _adaln_kernel(c_ref, w_ref, b_ref, o_ref):
    c = c_ref[...]
    c_act = (c * jax.nn.sigmoid(c)).astype(BF16)
    o_ref[...] = jnp.dot(c_act, w_ref[...].astype(BF16), preferred_element_type=F32) + b_ref[...]


def _adaln(c, ada_w, ada_b):
    depth, d, width = ada_w.shape
    bsz = c.shape[0]
    tn = 1536
    return pl.pallas_call(
        _adaln_kernel,
        grid=(depth, width // tn),
        in_specs=[
            pl.BlockSpec((bsz, d), lambda l, j: (0, 0)),
            pl.BlockSpec((None, d, tn), lambda l, j: (l, 0, j)),
            pl.BlockSpec((None, 1, tn), lambda l, j: (l, 0, j)),
        ],
        out_specs=pl.BlockSpec((None, bsz, tn), lambda l, j: (l, 0, j)),
        out_shape=jax.ShapeDtypeStruct((depth, bsz, width), F32),
        compiler_params=_params(("parallel", "parallel")),
        name="adaln",
    )(c, ada_w, ada_b.reshape(depth, 1, width))


def _modulated_norm(x, g, sc, sh):
    r = lax.rsqrt(jnp.mean(x * x, axis=-1, keepdims=True) + EPS)
    return x * r * (g * (1.0 + sc)) + sh


def _norm_proj_kernel(x_ref, g_ref, sc_ref, sh_ref, w_ref, o_ref, *, sub, col_chunks):
    hs = [_modulated_norm(x_ref[r:r + sub, :], g_ref[...], sc_ref[...], sh_ref[...]).astype(BF16)
          for r in range(0, x_ref.shape[0], sub)]
    for k, h in enumerate(hs):
        for lo, hi in col_chunks:
            o_ref[k * sub:(k + 1) * sub, lo:hi] = jnp.dot(
                h, w_ref[:, lo:hi], preferred_element_type=F32).astype(BF16)


def _norm_proj(x, norm_g, mod, w_in, layer, seq):
    n, d = x.shape
    width = w_in.shape[-1]
    bsz = n // seq
    tm = 1024
    per_seq = seq // tm
    tn = 512
    col_chunks = tuple((lo, min(lo + tn, width)) for lo in range(0, width, tn))

    def mod_idx(k):
        return lambda i: ((layer * bsz + i // per_seq) * N_MOD + k, 0, 0)

    return pl.pallas_call(
        functools.partial(_norm_proj_kernel, sub=256, col_chunks=col_chunks),
        grid=(n // tm,),
        in_specs=[
            pl.BlockSpec((tm, d), lambda i: (i, 0)),
            pl.BlockSpec((None, 1, d), lambda i: (layer, 0, 0)),
            pl.BlockSpec((None, 1, d), mod_idx(1)),
            pl.BlockSpec((None, 1, d), mod_idx(0)),
            pl.BlockSpec((None, d, width), lambda i: (layer, 0, 0)),
        ],
        out_specs=pl.BlockSpec((tm, width), lambda i: (i, 0)),
        out_shape=jax.ShapeDtypeStruct((n, width), BF16),
        compiler_params=_params(("parallel",)),
        name="norm_proj",
    )(x, norm_g, mod, mod, w_in)


ZERO_WEIGHT_LOG2 = -140.0


def _attn_kernel(q_ref, k_ref, v_ref, o_ref, qq_ref, carry_ref, acc_ref, nz_ref, w_ref,
                 more_ref, *, blk, pairs):
    win = blk
    rows = 2 * blk
    n_blocks = q_ref.shape[0] // blk
    head0 = lax.broadcasted_iota(jnp.int32, (blk, LANES), 1) < HEAD_DIM
    qscale = -(HEAD_DIM ** -0.5) * LOG2E
    k_max = jnp.max(jnp.max(jnp.abs(k_ref[...]), axis=0, keepdims=True).astype(F32))

    r = lax.broadcasted_iota(jnp.int32, (win, win), 0)
    c = lax.broadcasted_iota(jnp.int32, (win, win), 1)
    cum = jnp.where(r >= c, 1.0, 0.0).astype(BF16)

    t_row = lax.broadcasted_iota(jnp.int32, (rows, win), 0) & (blk - 1)
    s_col = lax.broadcasted_iota(jnp.int32, (rows, win), 1)
    causal = s_col < t_row
    ps = range(pairs)

    def key_start(w):
        return pl.multiple_of(jnp.maximum(w, 0) * win, win)

    def last_weights_times_v(last, slot, buf):
        start = key_start(last)
        for p in ps:
            acc_ref[buf, p] += jnp.dot(
                w_ref[slot, p], v_ref[pl.ds(start, win), p * LANES:(p + 1) * LANES],
                preferred_element_type=F32)

    def write_output(block, buf):
        out_rows = pl.ds(pl.multiple_of(block * blk, blk), blk)
        for p in ps:
            acc = acc_ref[buf, p]
            o_ref[out_rows, p * LANES:(p + 1) * LANES] = jnp.where(
                head0, acc[:blk], acc[blk:]).astype(BF16)

    def query_block(first, buf, pending):
        q_rows = pl.ds(pl.multiple_of(first * blk, blk), blk)
        q_l1 = jnp.zeros((), F32)
        for p in ps:
            q = q_ref[q_rows, p * LANES:(p + 1) * LANES].astype(F32) * qscale
            qq = jnp.concatenate([jnp.where(head0, q, 0.0),
                                  jnp.where(head0, 0.0, q)], axis=0).astype(BF16)
            qq_ref[p] = qq
            q_l1 = jnp.maximum(q_l1, jnp.max(jnp.sum(jnp.abs(qq.astype(F32)), axis=1)))
        stop_below = ZERO_WEIGHT_LOG2 - 2.0 ** -8 * (1.01 * q_l1 * k_max + 2.0)

        def scores(w, slot, p):
            start = key_start(w)
            nz_ref[slot, p] = lax.dot_general(
                qq_ref[p], k_ref[pl.ds(start, win), p * LANES:(p + 1) * LANES],
                (((1,), (1,)), ((), ())), preferred_element_type=F32)

        def weights_times_v(w, slot, p):
            start = key_start(w)
            acc_ref[buf, p] += jnp.dot(w_ref[slot, p],
                                       v_ref[pl.ds(start, win), p * LANES:(p + 1) * LANES],
                                       preferred_element_type=F32)

        def step(w, slot, diagonal=False):
            incl, carries, top = [], [], None
            for p in ps:
                nz = nz_ref[slot, p]
                sp = jnp.log(1.0 + jnp.exp2(-jnp.abs(nz))) * LOG2E
                log_stay = jnp.minimum(nz, 0.0) - sp
                if diagonal:
                    log_stay = jnp.where(causal, log_stay, 0.0)
                incl.append(jnp.dot(log_stay.astype(BF16), cum, preferred_element_type=F32))
                if not diagonal:
                    weights_times_v(w + 1, 1 - slot, p)
                scores(w - 1, 1 - slot, p)
                total = incl[p][:, 0:1]
                if diagonal:
                    carries.append(None)
                else:
                    carries.append(carry_ref[p])
                    total = carries[p] + total
                carry_ref[p] = total
                top = total if top is None else jnp.maximum(top, total)
            more_ref[0] = (jnp.max(top) > stop_below).astype(jnp.int32)
            for p in ps:
                if diagonal:
                    wt = jnp.where(causal, jnp.exp2(incl[p] - nz_ref[slot, p]), 0.0)
                else:
                    wt = jnp.exp2(incl[p] + carries[p] - nz_ref[slot, p])
                w_ref[slot, p] = wt.astype(BF16)

        def unfinished():
            return more_ref[0] == 1

        prev_block, prev_last, prev_slot = pending
        acc_ref[buf] = jnp.zeros(acc_ref.shape[1:], F32)
        for p in ps:
            scores(first, 0, p)
        last_weights_times_v(prev_last, prev_slot, 1 - buf)
        step(first, 0, diagonal=True)
        write_output(prev_block, 1 - buf)

        def body(state):
            w, _, _ = state
            step(w, 1)
            go_second = unfinished() & (w >= 1)

            @pl.when(go_second)
            def _():
                step(w - 1, 0)

            go_on = go_second & (w >= 2) & unfinished()
            return w - 2, go_on, jnp.where(go_second, w - 1, w)

        _, _, last = lax.while_loop(lambda state: state[1], body,
                                    (first - 1, (first >= 1) & unfinished(), first))
        return first, last, (first - last) & 1

    w_ref[...] = jnp.zeros_like(w_ref)
    acc_ref[...] = jnp.zeros_like(acc_ref)

    def two_blocks(m, pending):
        pending = query_block(2 * m, 0, pending)
        return query_block(2 * m + 1, 1, pending)

    zero = jnp.zeros((), jnp.int32)
    block, last, slot = lax.fori_loop(0, n_blocks // 2, two_blocks, (zero, zero, zero))
    last_weights_times_v(last, slot, 1)
    write_output(block, 1)


def _attention(proj, bsz, seq):
    width = proj.shape[-1]
    proj3 = proj.reshape(bsz, seq, width)
    blk = 256
    pairs = SB_WIDTH // LANES

    def column_block(k):
        return pl.BlockSpec((None, seq, SB_WIDTH), lambda b: (b, 0, k))

    return pl.pallas_call(
        functools.partial(_attn_kernel, blk=blk, pairs=pairs),
        grid=(bsz,),
        in_specs=[column_block(0), column_block(1), column_block(2)],
        out_specs=pl.BlockSpec((None, seq, SB_WIDTH), lambda b: (b, 0, 0)),
        out_shape=jax.ShapeDtypeStruct((bsz, seq, SB_WIDTH), BF16),
        scratch_shapes=[pltpu.VMEM((pairs, 2 * blk, LANES), BF16),
                        pltpu.VMEM((pairs, 2 * blk, 1), F32),
                        pltpu.VMEM((2, pairs, 2 * blk, LANES), F32),
                        pltpu.VMEM((2, pairs, 2 * blk, blk), F32),
                        pltpu.VMEM((2, pairs, 2 * blk, blk), BF16),
                        pltpu.SMEM((1,), jnp.int32)],
        compiler_params=_params(("parallel",)),
        name="sb_attention",
    )(proj3, proj3, proj3).reshape(bsz * seq, SB_WIDTH)


def _gelu(x):
    return 0.5 * x * (1.0 + jnp.tanh(0.7978845608028654 * (x + 0.044715 * (x * x * x))))


def _mix_mlp_kernel(a_ref, bg_ref, cg_ref, hc_ref, us_ref, vs_ref, cgh_ref, hch_ref,
                    x_ref, gm_ref, cw_ref, cb_ref, ng_ref, sw_ref, sbias_ref, wo_hbm,
                    g_ref, sc_ref, sh_ref, gf_ref, w1_hbm, w2_hbm, fg_ref,
                    o_ref, uext_ref, wo_ref, w1_ref, w2_ref, stage1_ref, stage2_ref, sem_ref, *,
                    layer, tm, per_seq, halo, tf, final_norm):
    i = pl.program_id(0)
    d, d_ff = w1_ref.shape
    n_chunks = d_ff // tf

    def w1_copy(f):
        return pltpu.make_async_copy(w1_hbm.at[layer, :, f * tf:(f + 1) * tf],
                                     stage1_ref.at[f % 2], sem_ref.at[0, f % 2])

    def w2_copy(f):
        return pltpu.make_async_copy(w2_hbm.at[layer, f * tf:(f + 1) * tf, :],
                                     stage2_ref.at[f % 2], sem_ref.at[1, f % 2])

    def wo_copy(r):
        return pltpu.make_async_copy(wo_hbm.at[layer, r * tf:(r + 1) * tf, :],
                                     stage2_ref.at[r], sem_ref.at[1, r])

    def tile(load_weights):
        if load_weights:
            for r in range(2):
                wo_copy(r).start()
                w1_copy(r).start()

        u = cg_ref[...].astype(F32) * hc_ref[...].astype(F32)
        u_halo = cgh_ref[...].astype(F32) * hch_ref[...].astype(F32)
        u_halo = jnp.where(i % per_seq == 0, 0.0, u_halo)
        uext_ref[0:halo, :] = u_halo
        uext_ref[halo:halo + tm, :] = u
        u_m1 = uext_ref[halo - 1:halo - 1 + tm, :]
        u_m2 = uext_ref[halo - 2:halo - 2 + tm, :]
        cw = cw_ref[...]
        y = cw[0:1] * u_m2 + cw[1:2] * u_m1 + cw[2:3] * u + cb_ref[...]
        c_out = bg_ref[...].astype(F32) * y

        ug = _gelu(us_ref[...].astype(F32))
        vg = _gelu(vs_ref[...].astype(F32))
        vn = vg * lax.rsqrt(jnp.mean(vg * vg, axis=-1, keepdims=True) + EPS) * ng_ref[...]
        t_chunk = lax.broadcasted_iota(jnp.int32, (SG_CHUNK, SG_CHUNK), 0) // CHUNK
        s_chunk = lax.broadcasted_iota(jnp.int32, (SG_CHUNK, SG_CHUNK), 1) // CHUNK
        chunk_causal = t_chunk >= s_chunk
        w_cat = jnp.concatenate(
            [jnp.where(chunk_causal, sw_ref[h], 0.0) for h in range(SG_HEADS)],
            axis=1).astype(BF16)
        lane_head = lax.broadcasted_iota(jnp.int32, (SG_CHUNK, SG_WIDTH), 1) // HEAD_DIM
        sbias = sbias_ref[...]
        mixed = []
        for n in range(tm // SG_CHUNK):
            v_win = vn[n * SG_CHUNK:(n + 1) * SG_CHUNK]
            v_stack = jnp.concatenate(
                [jnp.where(lane_head == h, v_win, 0.0) for h in range(SG_HEADS)],
                axis=0).astype(BF16)
            mixed.append(jnp.dot(w_cat, v_stack, preferred_element_type=F32) + sbias)
        s_out = ug * jnp.concatenate(mixed, axis=0)
        cat = jnp.concatenate([a_ref[...], c_out.astype(BF16), s_out.astype(BF16)], axis=1)

        if load_weights:
            for r in range(2):
                wo_copy(r).wait()
                wo_ref[r * tf:(r + 1) * tf, :] = stage2_ref[r].astype(BF16)
                w2_copy(r).start()
        mix = jnp.dot(cat, wo_ref[...], preferred_element_type=F32)
        x = x_ref[...] + gm_ref[...] * mix

        h = _modulated_norm(x, g_ref[...], sc_ref[...], sh_ref[...]).astype(BF16)
        acc = None
        for f in range(n_chunks):
            cols = slice(f * tf, (f + 1) * tf)
            if load_weights:
                w1_copy(f).wait()
                w1_ref[:, cols] = stage1_ref[f % 2].astype(BF16)
                w2_copy(f).wait()
                w2_ref[cols, :] = stage2_ref[f % 2].astype(BF16)
                if f + 2 < n_chunks:
                    w1_copy(f + 2).start()
                    w2_copy(f + 2).start()
            t = jnp.dot(h, w1_ref[:, cols], preferred_element_type=F32)
            t = jnp.square(jnp.maximum(t, 0.0)).astype(BF16)
            part = jnp.dot(t, w2_ref[cols, :], preferred_element_type=F32)
            acc = part if acc is None else acc + part
        y = x + gf_ref[...] * acc
        if final_norm:
            y = y * lax.rsqrt(jnp.mean(y * y, axis=-1, keepdims=True) + EPS) * fg_ref[...]
        o_ref[...] = y

    @pl.when(i == 0)
    def _():
        tile(load_weights=True)

    @pl.when(i > 0)
    def _():
        tile(load_weights=False)


def _mix_mlp(a_out, proj, x, mod, conv_w, conv_b, gmlp_norm_g, spatial_w, sbias, w_out,
             norm_g, w1, w2, final_g, layer, seq, final_norm):
    n, d = x.shape
    d_ff = w1.shape[-1]
    bsz = n // seq
    tm = 512
    tf = 512
    per_seq = seq // tm
    halo = BF16_SUBLANES
    col0 = 3 * SB_WIDTH // SC_WIDTH
    resident = pl.Buffered(1)

    def col(k):
        return pl.BlockSpec((tm, SC_WIDTH), lambda i: (i, col0 + k))

    def halo_col(k):
        return pl.BlockSpec((halo, SC_WIDTH),
                            lambda i: (jnp.maximum(i * (tm // halo) - 1, 0), col0 + k))

    def mod_row(k):
        return pl.BlockSpec((None, 1, d),
                            lambda i: ((layer * bsz + i // per_seq) * N_MOD + k, 0, 0))

    def layer_block(*shape):
        zeros = (0,) * len(shape)
        return pl.BlockSpec((None,) + shape, lambda i: (layer,) + zeros, pipeline_mode=resident)

    in_hbm = pl.BlockSpec(memory_space=pl.ANY)
    assert d == 2 * tf and d_ff % tf == 0

    return pl.pallas_call(
        functools.partial(_mix_mlp_kernel, layer=layer, tm=tm, per_seq=per_seq, halo=halo,
                          tf=tf, final_norm=final_norm),
        grid=(n // tm,),
        in_specs=[
            pl.BlockSpec((tm, SB_WIDTH), lambda i: (i, 0)),
            col(0), col(1), col(2), col(3), col(4),
            halo_col(1), halo_col(2),
            pl.BlockSpec((tm, d), lambda i: (i, 0)),
            mod_row(2),
            layer_block(3, SC_WIDTH),
            layer_block(1, SC_WIDTH),
            layer_block(1, SG_WIDTH),
            layer_block(SG_HEADS, SG_CHUNK, SG_CHUNK),
            layer_block(SG_CHUNK, SG_WIDTH),
            in_hbm,
            layer_block(1, d),
            mod_row(4), mod_row(3), mod_row(5),
            in_hbm,
            in_hbm,
            pl.BlockSpec((1, d), lambda i: (0, 0)),
        ],
        out_specs=pl.BlockSpec((tm, d), lambda i: (i, 0)),
        out_shape=jax.ShapeDtypeStruct((n, d), F32),
        scratch_shapes=[pltpu.VMEM((halo + tm, SC_WIDTH), F32),
                        pltpu.VMEM((d, d), BF16),
                        pltpu.VMEM((d, d_ff), BF16),
                        pltpu.VMEM((d_ff, d), BF16),
                        pltpu.VMEM((2, d, tf), F32),
                        pltpu.VMEM((2, tf, d), F32),
                        pltpu.SemaphoreType.DMA((2, 2))],
        compiler_params=_params(("arbitrary",)),
        name="mix_mlp",
    )(a_out, proj, proj, proj, proj, proj, proj, proj, x, mod,
      conv_w, conv_b, gmlp_norm_g, spatial_w, sbias, w_out,
      norm_g, mod, mod, mod, w1, w2, final_g)


def kernel(x, c, ada_w, ada_b, norm_mix_g, norm_mlp_g, w_in, conv_w, conv_b, gmlp_norm_g,
           spatial_w, spatial_b, w_out, mlp_w1, mlp_w2, final_norm_g):
    bsz, seq, d = x.shape
    depth = ada_w.shape[0]
    xf = x.reshape(bsz * seq, d)

    mod = _adaln(c, ada_w, ada_b).reshape(depth * bsz * N_MOD, 1, d)
    w_in_b = w_in.astype(BF16)
    norm_mix = norm_mix_g.reshape(depth, 1, d)
    norm_mlp = norm_mlp_g.reshape(depth, 1, d)
    conv_b3 = conv_b.reshape(depth, 1, SC_WIDTH)
    gmlp_g3 = gmlp_norm_g.reshape(depth, 1, SG_WIDTH)
    sbias = jnp.repeat(jnp.swapaxes(spatial_b, 1, 2), HEAD_DIM, axis=2)
    final_g = final_norm_g.reshape(1, d)

    for layer in range(depth):
        proj = _norm_proj(xf, norm_mix, mod, w_in_b, layer, seq)
        a_out = _attention(proj, bsz, seq)
        xf = _mix_mlp(a_out, proj, xf, mod, conv_w, conv_b3, gmlp_g3, spatial_w, sbias,
                      w_out, norm_mlp, mlp_w1, mlp_w2, final_g, layer, seq,
                      final_norm=(layer == depth - 1))
    return xf.reshape(bsz, seq, d)
```

```python
import functools

import jax
import jax.numpy as jnp
from jax import lax
from jax.experimental import pallas as pl
from jax.experimental.pallas import tpu as pltpu

F32 = jnp.float32
BF16 = jnp.bfloat16

HEAD_DIM = 64
SB_WIDTH = 512
SC_WIDTH = 256
SG_WIDTH = 256
SG_HEADS = 4
SG_CHUNK = 128
CHUNK = 64
N_MOD = 6
EPS = 1e-6
LANES = 128
BF16_SUBLANES = 16
LOG2E = 1.4426950408889634

VMEM_LIMIT = 48 * 1024 * 1024


def _params(sem):
    return pltpu.CompilerParams(dimension_semantics=sem, vmem_limit_bytes=VMEM_LIMIT)


def _adaln_kernel(c_ref, w_ref, b_ref, o_ref):
    c = c_ref[...]
    c_act = (c * jax.nn.sigmoid(c)).astype(BF16)
    o_ref[...] = jnp.dot(c_act, w_ref[...].astype(BF16), preferred_element_type=F32) + b_ref[...]


def _adaln(c, ada_w, ada_b):
    depth, d, width = ada_w.shape
    bsz = c.shape[0]
    tn = 1536
    return pl.pallas_call(
        _adaln_kernel,
        grid=(depth, width // tn),
        in_specs=[
            pl.BlockSpec((bsz, d), lambda l, j: (0, 0)),
            pl.BlockSpec((None, d, tn), lambda l, j: (l, 0, j)),
            pl.BlockSpec((None, 1, tn), lambda l, j: (l, 0, j)),
        ],
        out_specs=pl.BlockSpec((None, bsz, tn), lambda l, j: (l, 0, j)),
        out_shape=jax.ShapeDtypeStruct((depth, bsz, width), F32),
        compiler_params=_params(("parallel", "parallel")),
        name="adaln",
    )(c, ada_w, ada_b.reshape(depth, 1, width))


def _modulated_norm(x, g, sc, sh):
    r = lax.rsqrt(jnp.mean(x * x, axis=-1, keepdims=True) + EPS)
    return x * r * (g * (1.0 + sc)) + sh


def _norm_proj_kernel(x_ref, g_ref, sc_ref, sh_ref, w_ref, o_ref, *, sub, col_chunks):
    hs = [_modulated_norm(x_ref[r:r + sub, :], g_ref[...], sc_ref[...], sh_ref[...]).astype(BF16)
          for r in range(0, x_ref.shape[0], sub)]
    for k, h in enumerate(hs):
        for lo, hi in col_chunks:
            o_ref[k * sub:(k + 1) * sub, lo:hi] = jnp.dot(
                h, w_ref[:, lo:hi], preferred_element_type=F32).astype(BF16)


def _norm_proj(x, norm_g, mod, w_in, layer, seq):
    n, d = x.shape
    width = w_in.shape[-1]
    bsz = n // seq
    tm = 1024
    per_seq = seq // tm
    tn = 512
    col_chunks = tuple((lo, min(lo + tn, width)) for lo in range(0, width, tn))

    def mod_idx(k):
        return lambda i: ((layer * bsz + i // per_seq) * N_MOD + k, 0, 0)

    return pl.pallas_call(
        functools.partial(_norm_proj_kernel, sub=256, col_chunks=col_chunks),
        grid=(n // tm,),
        in_specs=[
            pl.BlockSpec((tm, d), lambda i: (i, 0)),
            pl.BlockSpec((None, 1, d), lambda i: (layer, 0, 0)),
            pl.BlockSpec((None, 1, d), mod_idx(1)),
            pl.BlockSpec((None, 1, d), mod_idx(0)),
            pl.BlockSpec((None, d, width), lambda i: (layer, 0, 0)),
        ],
        out_specs=pl.BlockSpec((tm, width), lambda i: (i, 0)),
        out_shape=jax.ShapeDtypeStruct((n, width), BF16),
        compiler_params=_params(("parallel",)),
        name="norm_proj",
    )(x, norm_g, mod, mod, w_in)


ZERO_WEIGHT_LOG2 = -140.0


def _attn_kernel(q_ref, k_ref, v_ref, o_ref, qq_ref, carry_ref, acc_ref, nz_ref, w_ref,
                 more_ref, *, blk, pairs):
    win = blk
    half = blk // 2
    n_blocks = q_ref.shape[0] // blk
    head0 = lax.broadcasted_iota(jnp.int32, (half, LANES), 1) < HEAD_DIM
    qscale = -(HEAD_DIM ** -0.5) * LOG2E
    k_max = jnp.max(jnp.max(jnp.abs(k_ref[...]), axis=0, keepdims=True).astype(F32))

    r = lax.broadcasted_iota(jnp.int32, (win, win), 0)
    c = lax.broadcasted_iota(jnp.int32, (win, win), 1)
    cum = jnp.where(r >= c, 1.0, 0.0).astype(BF16)

    t_lo = lax.broadcasted_iota(jnp.int32, (blk, half), 0) & (half - 1)
    causal_lo = lax.broadcasted_iota(jnp.int32, (blk, half), 1) < t_lo
    t_hi = half + (lax.broadcasted_iota(jnp.int32, (blk, win), 0) & (half - 1))
    causal_hi = lax.broadcasted_iota(jnp.int32, (blk, win), 1) < t_hi
    ps = range(pairs)

    def key_start(w):
        return pl.multiple_of(jnp.maximum(w, 0) * win, win)

    def last_weights_times_v(last, slot, buf):
        start = key_start(last)
        for p in ps:
            acc_ref[buf, p] += jnp.dot(
                w_ref[slot, p], v_ref[pl.ds(start, win), p * LANES:(p + 1) * LANES],
                preferred_element_type=F32)

    def write_output(block, buf):
        out_rows = pl.ds(pl.multiple_of(block * blk, blk), blk)
        for p in ps:
            acc = acc_ref[buf, p]
            o_ref[out_rows, p * LANES:(p + 1) * LANES] = jnp.concatenate(
                [jnp.where(head0, acc[:half], acc[half:blk]),
                 jnp.where(head0, acc[blk:blk + half], acc[blk + half:])], axis=0).astype(BF16)

    def query_block(first, buf, pending):
        q_rows = pl.ds(pl.multiple_of(first * blk, blk), blk)
        q_max = None
        for p in ps:
            q = q_ref[q_rows, p * LANES:(p + 1) * LANES].astype(F32) * qscale
            qq_ref[p] = jnp.concatenate(
                [jnp.where(head0, q[:half], 0.0), jnp.where(head0, 0.0, q[:half]),
                 jnp.where(head0, q[half:], 0.0), jnp.where(head0, 0.0, q[half:])],
                axis=0).astype(BF16)
            q_abs = jnp.abs(q)
            q_max = q_abs if q_max is None else jnp.maximum(q_max, q_abs)
        stop_below = ZERO_WEIGHT_LOG2 - 2.0 ** -8 * (
            1.02 * HEAD_DIM * jnp.max(q_max) * k_max + 2.0)

        def scores(w, slot, p):
            start = key_start(w)
            nz_ref[slot, p] = lax.dot_general(
                qq_ref[p], k_ref[pl.ds(start, win), p * LANES:(p + 1) * LANES],
                (((1,), (1,)), ((), ())), preferred_element_type=F32)

        def weights_times_v(w, slot, p):
            start = key_start(w)
            acc_ref[buf, p] += jnp.dot(w_ref[slot, p],
                                       v_ref[pl.ds(start, win), p * LANES:(p + 1) * LANES],
                                       preferred_element_type=F32)

        def log_stay_of(nz):
            return jnp.minimum(nz, 0.0) - jnp.log(1.0 + jnp.exp2(-jnp.abs(nz))) * LOG2E

        def cumulative(log_stay, width):
            return jnp.dot(log_stay.astype(BF16), cum[:width, :width],
                           preferred_element_type=F32)

        def step(w, slot):
            incl, carries, top = [], [], None
            for p in ps:
                incl.append(cumulative(log_stay_of(nz_ref[slot, p]), win))
                weights_times_v(w + 1, 1 - slot, p)
                scores(w - 1, 1 - slot, p)
                carries.append(carry_ref[p])
                total = carries[p] + incl[p][:, 0:1]
                carry_ref[p] = total
                top = total if top is None else jnp.maximum(top, total)
            more_ref[0] = (jnp.max(top) > stop_below).astype(jnp.int32)
            for p in ps:
                wt = jnp.exp2(incl[p] + carries[p] - nz_ref[slot, p])
                w_ref[slot, p] = wt.astype(BF16)

        def diagonal_step():
            incl, top = [], None
            for p in ps:
                lo = jnp.where(causal_lo, log_stay_of(nz_ref[0, p, :blk, :half]), 0.0)
                hi = jnp.where(causal_hi, log_stay_of(nz_ref[0, p, blk:, :]), 0.0)
                incl.append((cumulative(lo, half), cumulative(hi, win)))
                scores(first - 1, 1, p)
                total = jnp.concatenate([incl[p][0][:, 0:1], incl[p][1][:, 0:1]], axis=0)
                carry_ref[p] = total
                top = total if top is None else jnp.maximum(top, total)
            more_ref[0] = (jnp.max(top) > stop_below).astype(jnp.int32)
            for p in ps:
                lo = jnp.where(causal_lo, jnp.exp2(incl[p][0] - nz_ref[0, p, :blk, :half]), 0.0)
                hi = jnp.where(causal_hi, jnp.exp2(incl[p][1] - nz_ref[0, p, blk:, :]), 0.0)
                w_ref[0, p, :blk, :half] = lo.astype(BF16)
                w_ref[0, p, :blk, half:] = jnp.zeros((blk, half), BF16)
                w_ref[0, p, blk:, :] = hi.astype(BF16)

        def unfinished():
            return more_ref[0] == 1

        prev_block, prev_last, prev_slot = pending
        acc_ref[buf] = jnp.zeros(acc_ref.shape[1:], F32)
        for p in ps:
            scores(first, 0, p)
        last_weights_times_v(prev_last, prev_slot, 1 - buf)
        diagonal_step()
        write_output(prev_block, 1 - buf)

        def body(state):
            w, _, _ = state
            step(w, 1)
            go_second = unfinished() & (w >= 1)

            @pl.when(go_second)
            def _():
                step(w - 1, 0)

            go_on = go_second & (w >= 2) & unfinished()
            return w - 2, go_on, jnp.where(go_second, w - 1, w)

        _, _, last = lax.while_loop(lambda state: state[1], body,
                                    (first - 1, (first >= 1) & unfinished(), first))
        return first, last, (first - last) & 1

    w_ref[...] = jnp.zeros_like(w_ref)
    acc_ref[...] = jnp.zeros_like(acc_ref)

    def two_blocks(m, pending):
        pending = query_block(2 * m, 0, pending)
        return query_block(2 * m + 1, 1, pending)

    zero = jnp.zeros((), jnp.int32)
    block, last, slot = lax.fori_loop(0, n_blocks // 2, two_blocks, (zero, zero, zero))
    last_weights_times_v(last, slot, 1)
    write_output(block, 1)


def _attention(proj, bsz, seq):
    width = proj.shape[-1]
    proj3 = proj.reshape(bsz, seq, width)
    blk = 256
    pairs = SB_WIDTH // LANES

    def column_block(k):
        return pl.BlockSpec((None, seq, SB_WIDTH), lambda b: (b, 0, k))

    return pl.pallas_call(
        functools.partial(_attn_kernel, blk=blk, pairs=pairs),
        grid=(bsz,),
        in_specs=[column_block(0), column_block(1), column_block(2)],
        out_specs=pl.BlockSpec((None, seq, SB_WIDTH), lambda b: (b, 0, 0)),
        out_shape=jax.ShapeDtypeStruct((bsz, seq, SB_WIDTH), BF16),
        scratch_shapes=[pltpu.VMEM((pairs, 2 * blk, LANES), BF16),
                        pltpu.VMEM((pairs, 2 * blk, 1), F32),
                        pltpu.VMEM((2, pairs, 2 * blk, LANES), F32),
                        pltpu.VMEM((2, pairs, 2 * blk, blk), F32),
                        pltpu.VMEM((2, pairs, 2 * blk, blk), BF16),
                        pltpu.SMEM((1,), jnp.int32)],
        compiler_params=_params(("parallel",)),
        name="sb_attention",
    )(proj3, proj3, proj3).reshape(bsz * seq, SB_WIDTH)


def _gelu(x):
    return 0.5 * x * (1.0 + jnp.tanh(0.7978845608028654 * (x + 0.044715 * (x * x * x))))


def _mix_mlp_kernel(a_ref, bg_ref, cg_ref, hc_ref, us_ref, vs_ref, cgh_ref, hch_ref,
                    x_ref, gm_ref, cw_ref, cb_ref, ng_ref, sw_ref, sbias_ref, wo_hbm,
                    g_ref, sc_ref, sh_ref, gf_ref, w1_hbm, w2_hbm, fg_ref,
                    o_ref, uext_ref, wo_ref, w1_ref, w2_ref, stage1_ref, stage2_ref, sem_ref, *,
                    layer, tm, per_seq, halo, tf, final_norm):
    i = pl.program_id(0)
    d, d_ff = w1_ref.shape
    n_chunks = d_ff // tf

    def w1_copy(f):
        return pltpu.make_async_copy(w1_hbm.at[layer, :, f * tf:(f + 1) * tf],
                                     stage1_ref.at[f % 2], sem_ref.at[0, f % 2])

    def w2_copy(f):
        return pltpu.make_async_copy(w2_hbm.at[layer, f * tf:(f + 1) * tf, :],
                                     stage2_ref.at[f % 2], sem_ref.at[1, f % 2])

    def wo_copy(r):
        return pltpu.make_async_copy(wo_hbm.at[layer, r * tf:(r + 1) * tf, :],
                                     stage2_ref.at[r], sem_ref.at[1, r])

    def tile(load_weights):
        if load_weights:
            for r in range(2):
                wo_copy(r).start()
                w1_copy(r).start()

        u = cg_ref[...].astype(F32) * hc_ref[...].astype(F32)
        u_halo = cgh_ref[...].astype(F32) * hch_ref[...].astype(F32)
        u_halo = jnp.where(i % per_seq == 0, 0.0, u_halo)
        uext_ref[0:halo, :] = u_halo
        uext_ref[halo:halo + tm, :] = u
        u_m1 = uext_ref[halo - 1:halo - 1 + tm, :]
        u_m2 = uext_ref[halo - 2:halo - 2 + tm, :]
        cw = cw_ref[...]
        y = cw[0:1] * u_m2 + cw[1:2] * u_m1 + cw[2:3] * u + cb_ref[...]
        c_out = bg_ref[...].astype(F32) * y

        ug = _gelu(us_ref[...].astype(F32))
        vg = _gelu(vs_ref[...].astype(F32))
        vn = vg * lax.rsqrt(jnp.mean(vg * vg, axis=-1, keepdims=True) + EPS) * ng_ref[...]
        t_chunk = lax.broadcasted_iota(jnp.int32, (SG_CHUNK, SG_CHUNK), 0) // CHUNK
        s_chunk = lax.broadcasted_iota(jnp.int32, (SG_CHUNK, SG_CHUNK), 1) // CHUNK
        chunk_causal = t_chunk >= s_chunk
        w_cat = jnp.concatenate(
            [jnp.where(chunk_causal, sw_ref[h], 0.0) for h in range(SG_HEADS)],
            axis=1).astype(BF16)
        lane_head = lax.broadcasted_iota(jnp.int32, (SG_CHUNK, SG_WIDTH), 1) // HEAD_DIM
        sbias = sbias_ref[...]
        mixed = []
        for n in range(tm // SG_CHUNK):
            v_win = vn[n * SG_CHUNK:(n + 1) * SG_CHUNK]
            v_stack = jnp.concatenate(
                [jnp.where(lane_head == h, v_win, 0.0) for h in range(SG_HEADS)],
                axis=0).astype(BF16)
            mixed.append(jnp.dot(w_cat, v_stack, preferred_element_type=F32) + sbias)
        s_out = ug * jnp.concatenate(mixed, axis=0)
        cat = jnp.concatenate([a_ref[...], c_out.astype(BF16), s_out.astype(BF16)], axis=1)

        if load_weights:
            for r in range(2):
                wo_copy(r).wait()
                wo_ref[r * tf:(r + 1) * tf, :] = stage2_ref[r].astype(BF16)
                w2_copy(r).start()
        mix = jnp.dot(cat, wo_ref[...], preferred_element_type=F32)
        x = x_ref[...] + gm_ref[...] * mix

        h = _modulated_norm(x, g_ref[...], sc_ref[...], sh_ref[...]).astype(BF16)
        acc = None
        for f in range(n_chunks):
            cols = slice(f * tf, (f + 1) * tf)
            if load_weights:
                w1_copy(f).wait()
                w1_ref[:, cols] = stage1_ref[f % 2].astype(BF16)
                w2_copy(f).wait()
                w2_ref[cols, :] = stage2_ref[f % 2].astype(BF16)
                if f + 2 < n_chunks:
                    w1_copy(f + 2).start()
                    w2_copy(f + 2).start()
            t = jnp.dot(h, w1_ref[:, cols], preferred_element_type=F32)
            t = jnp.square(jnp.maximum(t, 0.0)).astype(BF16)
            part = jnp.dot(t, w2_ref[cols, :], preferred_element_type=F32)
            acc = part if acc is None else acc + part
        y = x + gf_ref[...] * acc
        if final_norm:
            y = y * lax.rsqrt(jnp.mean(y * y, axis=-1, keepdims=True) + EPS) * fg_ref[...]
        o_ref[...] = y

    @pl.when(i == 0)
    def _():
        tile(load_weights=True)

    @pl.when(i > 0)
    def _():
        tile(load_weights=False)


def _mix_mlp(a_out, proj, x, mod, conv_w, conv_b, gmlp_norm_g, spatial_w, sbias, w_out,
             norm_g, w1, w2, final_g, layer, seq, final_norm):
    n, d = x.shape
    d_ff = w1.shape[-1]
    bsz = n // seq
    tm = 512
    tf = 512
    per_seq = seq // tm
    halo = BF16_SUBLANES
    col0 = 3 * SB_WIDTH // SC_WIDTH
    resident = pl.Buffered(1)

    def col(k):
        return pl.BlockSpec((tm, SC_WIDTH), lambda i: (i, col0 + k))

    def halo_col(k):
        return pl.BlockSpec((halo, SC_WIDTH),
                            lambda i: (jnp.maximum(i * (tm // halo) - 1, 0), col0 + k))

    def mod_row(k):
        return pl.BlockSpec((None, 1, d),
                            lambda i: ((layer * bsz + i // per_seq) * N_MOD + k, 0, 0))

    def layer_block(*shape):
        zeros = (0,) * len(shape)
        return pl.BlockSpec((None,) + shape, lambda i: (layer,) + zeros, pipeline_mode=resident)

    in_hbm = pl.BlockSpec(memory_space=pl.ANY)
    assert d == 2 * tf and d_ff % tf == 0

    return pl.pallas_call(
        functools.partial(_mix_mlp_kernel, layer=layer, tm=tm, per_seq=per_seq, halo=halo,
                          tf=tf, final_norm=final_norm),
        grid=(n // tm,),
        in_specs=[
            pl.BlockSpec((tm, SB_WIDTH), lambda i: (i, 0)),
            col(0), col(1), col(2), col(3), col(4),
            halo_col(1), halo_col(2),
            pl.BlockSpec((tm, d), lambda i: (i, 0)),
            mod_row(2),
            layer_block(3, SC_WIDTH),
            layer_block(1, SC_WIDTH),
            layer_block(1, SG_WIDTH),
            layer_block(SG_HEADS, SG_CHUNK, SG_CHUNK),
            layer_block(SG_CHUNK, SG_WIDTH),
            in_hbm,
            layer_block(1, d),
            mod_row(4), mod_row(3), mod_row(5),
            in_hbm,
            in_hbm,
            pl.BlockSpec((1, d), lambda i: (0, 0)),
        ],
        out_specs=pl.BlockSpec((tm, d), lambda i: (i, 0)),
        out_shape=jax.ShapeDtypeStruct((n, d), F32),
        scratch_shapes=[pltpu.VMEM((halo + tm, SC_WIDTH), F32),
                        pltpu.VMEM((d, d), BF16),
                        pltpu.VMEM((d, d_ff), BF16),
                        pltpu.VMEM((d_ff, d), BF16),
                        pltpu.VMEM((2, d, tf), F32),
                        pltpu.VMEM((2, tf, d), F32),
                        pltpu.SemaphoreType.DMA((2, 2))],
        compiler_params=_params(("arbitrary",)),
        name="mix_mlp",
    )(a_out, proj, proj, proj, proj, proj, proj, proj, x, mod,
      conv_w, conv_b, gmlp_norm_g, spatial_w, sbias, w_out,
      norm_g, mod, mod, mod, w1, w2, final_g)


def kernel(x, c, ada_w, ada_b, norm_mix_g, norm_mlp_g, w_in, conv_w, conv_b, gmlp_norm_g,
           spatial_w, spatial_b, w_out, mlp_w1, mlp_w2, final_norm_g):
    bsz, seq, d = x.shape
    depth = ada_w.shape[0]
    xf = x.reshape(bsz * seq, d)

    mod = _adaln(c, ada_w, ada_b).reshape(depth * bsz * N_MOD, 1, d)
    w_in_b = w_in.astype(BF16)
    norm_mix = norm_mix_g.reshape(depth, 1, d)
    norm_mlp = norm_mlp_g.reshape(depth, 1, d)
    conv_b3 = conv_b.reshape(depth, 1, SC_WIDTH)
    gmlp_g3 = gmlp_norm_g.reshape(depth, 1, SG_WIDTH)
    sbias = jnp.repeat(jnp.swapaxes(spatial_b, 1, 2), HEAD_DIM, axis=2)
    final_g = final_norm_g.reshape(1, d)

    for layer in range(depth):
        proj = _norm_proj(xf, norm_mix, mod, w_in_b, layer, seq)
        a_out = _attention(proj, bsz, seq)
        xf = _mix_mlp(a_out, proj, xf, mod, conv_w, conv_b3, gmlp_g3, spatial_w, sbias,
                      w_out, norm_mlp, mlp_w1, mlp_w2, final_g, layer, seq,
                      final_norm=(layer == depth - 1))
    return xf.reshape(bsz, seq, d)
```

```python
import functools

import jax
import jax.numpy as jnp
from jax import lax
from jax.experimental import pallas as pl
from jax.experimental.pallas import tpu as pltpu

F32 = jnp.float32
BF16 = jnp.bfloat16

HEAD_DIM = 64
SB_WIDTH = 512
SC_WIDTH = 256
SG_WIDTH = 256
SG_HEADS = 4
SG_CHUNK = 128
CHUNK = 64
N_MOD = 6
EPS = 1e-6
LANES = 128
BF16_SUBLANES = 16
LOG2E = 1.4426950408889634

VMEM_LIMIT = 48 * 1024 * 1024


def _params(sem):
    return pltpu.CompilerParams(dimension_semantics=sem, vmem_limit_bytes=VMEM_LIMIT)


def _adaln_kernel(c_ref, w_ref, b_ref, o_ref):
    c = c_ref[...]
    c_act = (c * jax.nn.sigmoid(c)).astype(BF16)
    o_ref[...] = jnp.dot(c_act, w_ref[...].astype(BF16), preferred_element_type=F32) + b_ref[...]


def _adaln(c, ada_w, ada_b):
    depth, d, width = ada_w.shape
    bsz = c.shape[0]
    tn = 1536
    return pl.pallas_call(
        _adaln_kernel,
        grid=(depth, width // tn),
        in_specs=[
            pl.BlockSpec((bsz, d), lambda l, j: (0, 0)),
            pl.BlockSpec((None, d, tn), lambda l, j: (l, 0, j)),
            pl.BlockSpec((None, 1, tn), lambda l, j: (l, 0, j)),
        ],
        out_specs=pl.BlockSpec((None, bsz, tn), lambda l, j: (l, 0, j)),
        out_shape=jax.ShapeDtypeStruct((depth, bsz, width), F32),
        compiler_params=_params(("parallel", "parallel")),
        name="adaln",
    )(c, ada_w, ada_b.reshape(depth, 1, width))


def _modulated_norm(x, g, sc, sh):
    r = lax.rsqrt(jnp.mean(x * x, axis=-1, keepdims=True) + EPS)
    return x * r * (g * (1.0 + sc)) + sh


def _gelu(x):
    return 0.5 * x * (1.0 + jnp.tanh(0.7978845608028654 * (x + 0.044715 * (x * x * x))))


def _norm_proj_kernel(x_ref, g_ref, sc_ref, sh_ref, ng_ref, w_ref, o_ref, *, sub):
    hs = [_modulated_norm(x_ref[r:r + sub, :], g_ref[...], sc_ref[...], sh_ref[...]).astype(BF16)
          for r in range(0, x_ref.shape[0], sub)]
    qkv = 3 * SB_WIDTH
    for k, h in enumerate(hs):
        rows = slice(k * sub, (k + 1) * sub)

        def columns(lo, width):
            return jnp.dot(h, w_ref[:, lo:lo + width], preferred_element_type=F32)

        gates = columns(qkv, 2 * SC_WIDTH)
        conv_in_u = columns(qkv + 2 * SC_WIDTH, SC_WIDTH + SG_WIDTH)
        vg = _gelu(columns(qkv + 3 * SC_WIDTH + SG_WIDTH, SG_WIDTH))
        vn = vg * lax.rsqrt(jnp.mean(vg * vg, axis=-1, keepdims=True) + EPS) * ng_ref[...]
        o_ref[rows, qkv:qkv + SC_WIDTH] = gates[:, :SC_WIDTH].astype(BF16)
        o_ref[rows, qkv + SC_WIDTH:qkv + 2 * SC_WIDTH] = (
            gates[:, SC_WIDTH:] * conv_in_u[:, :SC_WIDTH]).astype(BF16)
        o_ref[rows, qkv + 2 * SC_WIDTH:qkv + 3 * SC_WIDTH] = _gelu(
            conv_in_u[:, SC_WIDTH:]).astype(BF16)
        o_ref[rows, qkv + 3 * SC_WIDTH:qkv + 4 * SC_WIDTH] = vn.astype(BF16)
        for lo in range(0, qkv, SB_WIDTH):
            o_ref[rows, lo:lo + SB_WIDTH] = columns(lo, SB_WIDTH).astype(BF16)


def _norm_proj(x, norm_g, mod, gmlp_norm_g, w_in, layer, seq):
    n, d = x.shape
    width = w_in.shape[-1]
    out_width = 3 * SB_WIDTH + 2 * SC_WIDTH + 2 * SG_WIDTH
    bsz = n // seq
    tm = 1024
    per_seq = seq // tm

    def mod_idx(k):
        return lambda i: ((layer * bsz + i // per_seq) * N_MOD + k, 0, 0)

    return pl.pallas_call(
        functools.partial(_norm_proj_kernel, sub=256),
        grid=(n // tm,),
        in_specs=[
            pl.BlockSpec((tm, d), lambda i: (i, 0)),
            pl.BlockSpec((None, 1, d), lambda i: (layer, 0, 0)),
            pl.BlockSpec((None, 1, d), mod_idx(1)),
            pl.BlockSpec((None, 1, d), mod_idx(0)),
            pl.BlockSpec((None, 1, SG_WIDTH), lambda i: (layer, 0, 0)),
            pl.BlockSpec((None, d, width), lambda i: (layer, 0, 0)),
        ],
        out_specs=pl.BlockSpec((tm, out_width), lambda i: (i, 0)),
        out_shape=jax.ShapeDtypeStruct((n, out_width), BF16),
        compiler_params=_params(("parallel",)),
        name="norm_proj",
    )(x, norm_g, mod, mod, gmlp_norm_g, w_in)


ZERO_WEIGHT_LOG2 = -140.0


def _attn_kernel(q_ref, k_ref, v_ref, o_ref, qq_ref, carry_ref, acc_ref, nz_ref, w_ref,
                 more_ref, *, blk, pairs):
    win = blk
    half = blk // 2
    n_blocks = q_ref.shape[0] // blk
    head0 = lax.broadcasted_iota(jnp.int32, (half, LANES), 1) < HEAD_DIM
    qscale = -(HEAD_DIM ** -0.5) * LOG2E
    k_max = jnp.max(jnp.max(jnp.abs(k_ref[...]), axis=0, keepdims=True).astype(F32))

    r = lax.broadcasted_iota(jnp.int32, (win, win), 0)
    c = lax.broadcasted_iota(jnp.int32, (win, win), 1)
    cum = jnp.where(r >= c, 1.0, 0.0).astype(BF16)

    t_lo = lax.broadcasted_iota(jnp.int32, (blk, half), 0) & (half - 1)
    causal_lo = lax.broadcasted_iota(jnp.int32, (blk, half), 1) < t_lo
    t_hi = half + (lax.broadcasted_iota(jnp.int32, (blk, win), 0) & (half - 1))
    causal_hi = lax.broadcasted_iota(jnp.int32, (blk, win), 1) < t_hi
    ps = range(pairs)

    def key_start(w):
        return pl.multiple_of(jnp.maximum(w, 0) * win, win)

    def last_weights_times_v(last, slot, buf):
        start = key_start(last)
        for p in ps:
            acc_ref[buf, p] += jnp.dot(
                w_ref[slot, p], v_ref[pl.ds(start, win), p * LANES:(p + 1) * LANES],
                preferred_element_type=F32)

    def write_output(block, buf):
        out_rows = pl.ds(pl.multiple_of(block * blk, blk), blk)
        for p in ps:
            acc = acc_ref[buf, p]
            o_ref[out_rows, p * LANES:(p + 1) * LANES] = jnp.concatenate(
                [jnp.where(head0, acc[:half], acc[half:blk]),
                 jnp.where(head0, acc[blk:blk + half], acc[blk + half:])], axis=0).astype(BF16)

    def query_block(first, buf, pending):
        q_rows = pl.ds(pl.multiple_of(first * blk, blk), blk)
        q_max = None
        for p in ps:
            q = q_ref[q_rows, p * LANES:(p + 1) * LANES].astype(F32) * qscale
            qq_ref[p] = jnp.concatenate(
                [jnp.where(head0, q[:half], 0.0), jnp.where(head0, 0.0, q[:half]),
                 jnp.where(head0, q[half:], 0.0), jnp.where(head0, 0.0, q[half:])],
                axis=0).astype(BF16)
            q_abs = jnp.abs(q)
            q_max = q_abs if q_max is None else jnp.maximum(q_max, q_abs)
        stop_below = ZERO_WEIGHT_LOG2 - 2.0 ** -8 * (
            1.02 * HEAD_DIM * jnp.max(q_max) * k_max + 2.0)

        def scores(w, slot, p):
            start = key_start(w)
            nz_ref[slot, p] = lax.dot_general(
                qq_ref[p], k_ref[pl.ds(start, win), p * LANES:(p + 1) * LANES],
                (((1,), (1,)), ((), ())), preferred_element_type=F32)

        def weights_times_v(w, slot, p):
            start = key_start(w)
            acc_ref[buf, p] += jnp.dot(w_ref[slot, p],
                                       v_ref[pl.ds(start, win), p * LANES:(p + 1) * LANES],
                                       preferred_element_type=F32)

        def log_stay_of(nz):
            return jnp.minimum(nz, 0.0) - jnp.log(1.0 + jnp.exp2(-jnp.abs(nz))) * LOG2E

        def cumulative(log_stay, width):
            return jnp.dot(log_stay.astype(BF16), cum[:width, :width],
                           preferred_element_type=F32)

        def step(w, slot):
            incl, carries, top = [], [], None
            for p in ps:
                incl.append(cumulative(log_stay_of(nz_ref[slot, p]), win))
                weights_times_v(w + 1, 1 - slot, p)
                scores(w - 1, 1 - slot, p)
                carries.append(carry_ref[p])
                total = carries[p] + incl[p][:, 0:1]
                carry_ref[p] = total
                top = total if top is None else jnp.maximum(top, total)
            more_ref[0] = (jnp.max(top) > stop_below).astype(jnp.int32)
            for p in ps:
                wt = jnp.exp2(incl[p] + carries[p] - nz_ref[slot, p])
                w_ref[slot, p] = wt.astype(BF16)

        def diagonal_step():
            incl, top = [], None
            for p in ps:
                lo = jnp.where(causal_lo, log_stay_of(nz_ref[0, p, :blk, :half]), 0.0)
                hi = jnp.where(causal_hi, log_stay_of(nz_ref[0, p, blk:, :]), 0.0)
                incl.append((cumulative(lo, half), cumulative(hi, win)))
                scores(first - 1, 1, p)
                total = jnp.concatenate([incl[p][0][:, 0:1], incl[p][1][:, 0:1]], axis=0)
                carry_ref[p] = total
                top = total if top is None else jnp.maximum(top, total)
            more_ref[0] = (jnp.max(top) > stop_below).astype(jnp.int32)
            for p in ps:
                lo = jnp.where(causal_lo, jnp.exp2(incl[p][0] - nz_ref[0, p, :blk, :half]), 0.0)
                hi = jnp.where(causal_hi, jnp.exp2(incl[p][1] - nz_ref[0, p, blk:, :]), 0.0)
                w_ref[0, p, :blk, :half] = lo.astype(BF16)
                w_ref[0, p, :blk, half:] = jnp.zeros((blk, half), BF16)
                w_ref[0, p, blk:, :] = hi.astype(BF16)

        def unfinished():
            return more_ref[0] == 1

        prev_block, prev_last, prev_slot = pending
        acc_ref[buf] = jnp.zeros(acc_ref.shape[1:], F32)
        for p in ps:
            scores(first, 0, p)
        last_weights_times_v(prev_last, prev_slot, 1 - buf)
        diagonal_step()
        write_output(prev_block, 1 - buf)

        def body(state):
            w, _, _ = state
            step(w, 1)
            go_second = unfinished() & (w >= 1)

            @pl.when(go_second)
            def _():
                step(w - 1, 0)

            go_on = go_second & (w >= 2) & unfinished()
            return w - 2, go_on, jnp.where(go_second, w - 1, w)

        _, _, last = lax.while_loop(lambda state: state[1], body,
                                    (first - 1, (first >= 1) & unfinished(), first))
        return first, last, (first - last) & 1

    w_ref[...] = jnp.zeros_like(w_ref)
    acc_ref[...] = jnp.zeros_like(acc_ref)

    def two_blocks(m, pending):
        pending = query_block(2 * m, 0, pending)
        return query_block(2 * m + 1, 1, pending)

    zero = jnp.zeros((), jnp.int32)
    block, last, slot = lax.fori_loop(0, n_blocks // 2, two_blocks, (zero, zero, zero))
    last_weights_times_v(last, slot, 1)
    write_output(block, 1)


def _attention(proj, bsz, seq):
    width = proj.shape[-1]
    proj3 = proj.reshape(bsz, seq, width)
    blk = 256
    pairs = SB_WIDTH // LANES

    def column_block(k):
        return pl.BlockSpec((None, seq, SB_WIDTH), lambda b: (b, 0, k))

    return pl.pallas_call(
        functools.partial(_attn_kernel, blk=blk, pairs=pairs),
        grid=(bsz,),
        in_specs=[column_block(0), column_block(1), column_block(2)],
        out_specs=pl.BlockSpec((None, seq, SB_WIDTH), lambda b: (b, 0, 0)),
        out_shape=jax.ShapeDtypeStruct((bsz, seq, SB_WIDTH), BF16),
        scratch_shapes=[pltpu.VMEM((pairs, 2 * blk, LANES), BF16),
                        pltpu.VMEM((pairs, 2 * blk, 1), F32),
                        pltpu.VMEM((2, pairs, 2 * blk, LANES), F32),
                        pltpu.VMEM((2, pairs, 2 * blk, blk), F32),
                        pltpu.VMEM((2, pairs, 2 * blk, blk), BF16),
                        pltpu.SMEM((1,), jnp.int32)],
        compiler_params=_params(("parallel",)),
        name="sb_attention",
    )(proj3, proj3, proj3).reshape(bsz * seq, SB_WIDTH)


def _mix_mlp_kernel(a_ref, bg_ref, u_ref, ug_ref, vn_ref, u_halo_ref,
                    x_ref, gm_ref, cw_ref, cb_ref, sw_ref, sbias_ref, wo_hbm,
                    g_ref, sc_ref, sh_ref, gf_ref, w1_hbm, w2_hbm, fg_ref,
                    o_ref, uext_ref, wo_ref, w1_ref, w2_ref, stage1_ref, stage2_ref, sem_ref, *,
                    layer, tm, per_seq, halo, tf, final_norm):
    i = pl.program_id(0)
    d, d_ff = w1_ref.shape
    n_chunks = d_ff // tf

    def w1_copy(f):
        return pltpu.make_async_copy(w1_hbm.at[layer, :, f * tf:(f + 1) * tf],
                                     stage1_ref.at[f % 2], sem_ref.at[0, f % 2])

    def w2_copy(f):
        return pltpu.make_async_copy(w2_hbm.at[layer, f * tf:(f + 1) * tf, :],
                                     stage2_ref.at[f % 2], sem_ref.at[1, f % 2])

    def wo_copy(r):
        return pltpu.make_async_copy(wo_hbm.at[layer, r * tf:(r + 1) * tf, :],
                                     stage2_ref.at[r], sem_ref.at[1, r])

    def tile(load_weights):
        if load_weights:
            for r in range(2):
                wo_copy(r).start()
                w1_copy(r).start()

        u = u_ref[...].astype(F32)
        u_halo = jnp.where(i % per_seq == 0, 0.0, u_halo_ref[...].astype(F32))
        uext_ref[0:halo, :] = u_halo
        uext_ref[halo:halo + tm, :] = u
        u_m1 = uext_ref[halo - 1:halo - 1 + tm, :]
        u_m2 = uext_ref[halo - 2:halo - 2 + tm, :]
        cw = cw_ref[...]
        y = cw[0:1] * u_m2 + cw[1:2] * u_m1 + cw[2:3] * u + cb_ref[...]
        c_out = bg_ref[...].astype(F32) * y

        ug = ug_ref[...].astype(F32)
        vn = vn_ref[...].astype(F32)
        t_chunk = lax.broadcasted_iota(jnp.int32, (SG_CHUNK, SG_CHUNK), 0) // CHUNK
        s_chunk = lax.broadcasted_iota(jnp.int32, (SG_CHUNK, SG_CHUNK), 1) // CHUNK
        chunk_causal = t_chunk >= s_chunk
        w_cat = jnp.concatenate(
            [jnp.where(chunk_causal, sw_ref[h], 0.0) for h in range(SG_HEADS)],
            axis=1).astype(BF16)
        lane_head = lax.broadcasted_iota(jnp.int32, (SG_CHUNK, SG_WIDTH), 1) // HEAD_DIM
        sbias = sbias_ref[...]
        mixed = []
        for n in range(tm // SG_CHUNK):
            v_win = vn[n * SG_CHUNK:(n + 1) * SG_CHUNK]
            v_stack = jnp.concatenate(
                [jnp.where(lane_head == h, v_win, 0.0) for h in range(SG_HEADS)],
                axis=0).astype(BF16)
            mixed.append(jnp.dot(w_cat, v_stack, preferred_element_type=F32) + sbias)
        s_out = ug * jnp.concatenate(mixed, axis=0)
        cat = jnp.concatenate([a_ref[...], c_out.astype(BF16), s_out.astype(BF16)], axis=1)

        if load_weights:
            for r in range(2):
                wo_copy(r).wait()
                wo_ref[r * tf:(r + 1) * tf, :] = stage2_ref[r].astype(BF16)
                w2_copy(r).start()
        mix = jnp.dot(cat, wo_ref[...], preferred_element_type=F32)
        x = x_ref[...] + gm_ref[...] * mix

        h = _modulated_norm(x, g_ref[...], sc_ref[...], sh_ref[...]).astype(BF16)
        acc = None
        for f in range(n_chunks):
            cols = slice(f * tf, (f + 1) * tf)
            if load_weights:
                w1_copy(f).wait()
                w1_ref[:, cols] = stage1_ref[f % 2].astype(BF16)
                w2_copy(f).wait()
                w2_ref[cols, :] = stage2_ref[f % 2].astype(BF16)
                if f + 2 < n_chunks:
                    w1_copy(f + 2).start()
                    w2_copy(f + 2).start()
            t = jnp.dot(h, w1_ref[:, cols], preferred_element_type=F32)
            t = jnp.square(jnp.maximum(t, 0.0)).astype(BF16)
            part = jnp.dot(t, w2_ref[cols, :], preferred_element_type=F32)
            acc = part if acc is None else acc + part
        y = x + gf_ref[...] * acc
        if final_norm:
            y = y * lax.rsqrt(jnp.mean(y * y, axis=-1, keepdims=True) + EPS) * fg_ref[...]
        o_ref[...] = y

    @pl.when(i == 0)
    def _():
        tile(load_weights=True)

    @pl.when(i > 0)
    def _():
        tile(load_weights=False)


def _mix_mlp(a_out, proj, x, mod, conv_w, conv_b, spatial_w, sbias, w_out,
             norm_g, w1, w2, final_g, layer, seq, final_norm):
    n, d = x.shape
    d_ff = w1.shape[-1]
    bsz = n // seq
    tm = 512
    tf = 512
    per_seq = seq // tm
    halo = BF16_SUBLANES
    col0 = 3 * SB_WIDTH // SC_WIDTH
    resident = pl.Buffered(1)

    def col(k):
        return pl.BlockSpec((tm, SC_WIDTH), lambda i: (i, col0 + k))

    def halo_col(k):
        return pl.BlockSpec((halo, SC_WIDTH),
                            lambda i: (jnp.maximum(i * (tm // halo) - 1, 0), col0 + k))

    def mod_row(k):
        return pl.BlockSpec((None, 1, d),
                            lambda i: ((layer * bsz + i // per_seq) * N_MOD + k, 0, 0))

    def layer_block(*shape):
        zeros = (0,) * len(shape)
        return pl.BlockSpec((None,) + shape, lambda i: (layer,) + zeros, pipeline_mode=resident)

    in_hbm = pl.BlockSpec(memory_space=pl.ANY)
    assert d == 2 * tf and d_ff % tf == 0

    return pl.pallas_call(
        functools.partial(_mix_mlp_kernel, layer=layer, tm=tm, per_seq=per_seq, halo=halo,
                          tf=tf, final_norm=final_norm),
        grid=(n // tm,),
        in_specs=[
            pl.BlockSpec((tm, SB_WIDTH), lambda i: (i, 0)),
            col(0), col(1), col(2), col(3),
            halo_col(1),
            pl.BlockSpec((tm, d), lambda i: (i, 0)),
            mod_row(2),
            layer_block(3, SC_WIDTH),
            layer_block(1, SC_WIDTH),
            layer_block(SG_HEADS, SG_CHUNK, SG_CHUNK),
            layer_block(SG_CHUNK, SG_WIDTH),
            in_hbm,
            layer_block(1, d),
            mod_row(4), mod_row(3), mod_row(5),
            in_hbm,
            in_hbm,
            pl.BlockSpec((1, d), lambda i: (0, 0)),
        ],
        out_specs=pl.BlockSpec((tm, d), lambda i: (i, 0)),
        out_shape=jax.ShapeDtypeStruct((n, d), F32),
        scratch_shapes=[pltpu.VMEM((halo + tm, SC_WIDTH), F32),
                        pltpu.VMEM((d, d), BF16),
                        pltpu.VMEM((d, d_ff), BF16),
                        pltpu.VMEM((d_ff, d), BF16),
                        pltpu.VMEM((2, d, tf), F32),
                        pltpu.VMEM((2, tf, d), F32),
                        pltpu.SemaphoreType.DMA((2, 2))],
        compiler_params=_params(("arbitrary",)),
        name="mix_mlp",
    )(a_out, proj, proj, proj, proj, proj, x, mod,
      conv_w, conv_b, spatial_w, sbias, w_out,
      norm_g, mod, mod, mod, w1, w2, final_g)


def kernel(x, c, ada_w, ada_b, norm_mix_g, norm_mlp_g, w_in, conv_w, conv_b, gmlp_norm_g,
           spatial_w, spatial_b, w_out, mlp_w1, mlp_w2, final_norm_g):
    bsz, seq, d = x.shape
    depth = ada_w.shape[0]
    xf = x.reshape(bsz * seq, d)

    mod = _adaln(c, ada_w, ada_b).reshape(depth * bsz * N_MOD, 1, d)
    w_in_b = w_in.astype(BF16)
    norm_mix = norm_mix_g.reshape(depth, 1, d)
    norm_mlp = norm_mlp_g.reshape(depth, 1, d)
    conv_b3 = conv_b.reshape(depth, 1, SC_WIDTH)
    gmlp_g3 = gmlp_norm_g.reshape(depth, 1, SG_WIDTH)
    sbias = jnp.repeat(jnp.swapaxes(spatial_b, 1, 2), HEAD_DIM, axis=2)
    final_g = final_norm_g.reshape(1, d)

    for layer in range(depth):
        proj = _norm_proj(xf, norm_mix, mod, gmlp_g3, w_in_b, layer, seq)
        a_out = _attention(proj, bsz, seq)
        xf = _mix_mlp(a_out, proj, xf, mod, conv_w, conv_b3, spatial_w, sbias,
                      w_out, norm_mlp, mlp_w1, mlp_w2, final_g, layer, seq,
                      final_norm=(layer == depth - 1))
    return xf.reshape(bsz, seq, d)
```

```python
import functools

import jax
import jax.numpy as jnp
from jax import lax
from jax.experimental import pallas as pl
from jax.experimental.pallas import tpu as pltpu

F32 = jnp.float32
BF16 = jnp.bfloat16

HEAD_DIM = 64
SB_WIDTH = 512
SC_WIDTH = 256
SG_WIDTH = 256
SG_HEADS = 4
SG_CHUNK = 128
CHUNK = 64
N_MOD = 6
EPS = 1e-6
LANES = 128
BF16_SUBLANES = 16
LOG2E = 1.4426950408889634

VMEM_LIMIT = 52 * 1024 * 1024


def _params(sem):
    return pltpu.CompilerParams(dimension_semantics=sem, vmem_limit_bytes=VMEM_LIMIT)


def _adaln_kernel(c_ref, w_ref, b_ref, o_ref):
    c = c_ref[...]
    c_act = (c * jax.nn.sigmoid(c)).astype(BF16)
    o_ref[...] = jnp.dot(c_act, w_ref[...].astype(BF16), preferred_element_type=F32) + b_ref[...]


def _adaln(c, ada_w, ada_b):
    depth, d, width = ada_w.shape
    bsz = c.shape[0]
    tn = 1536
    return pl.pallas_call(
        _adaln_kernel,
        grid=(depth, width // tn),
        in_specs=[
            pl.BlockSpec((bsz, d), lambda l, j: (0, 0)),
            pl.BlockSpec((None, d, tn), lambda l, j: (l, 0, j)),
            pl.BlockSpec((None, 1, tn), lambda l, j: (l, 0, j)),
        ],
        out_specs=pl.BlockSpec((None, bsz, tn), lambda l, j: (l, 0, j)),
        out_shape=jax.ShapeDtypeStruct((depth, bsz, width), F32),
        compiler_params=_params(("parallel", "parallel")),
        name="adaln",
    )(c, ada_w, ada_b.reshape(depth, 1, width))


def _modulated_norm(x, g, sc, sh):
    r = lax.rsqrt(jnp.mean(x * x, axis=-1, keepdims=True) + EPS)
    return x * r * (g * (1.0 + sc)) + sh


def _norm_proj_kernel(x_ref, g_ref, sc_ref, sh_ref, w_ref, o_ref, *, sub, col_chunks):
    hs = [_modulated_norm(x_ref[r:r + sub, :], g_ref[...], sc_ref[...], sh_ref[...]).astype(BF16)
          for r in range(0, x_ref.shape[0], sub)]
    for k, h in enumerate(hs):
        for lo, hi in col_chunks:
            o_ref[k * sub:(k + 1) * sub, lo:hi] = jnp.dot(
                h, w_ref[:, lo:hi], preferred_element_type=F32).astype(BF16)


def _norm_proj(x, norm_g, mod, w_in, layer, seq):
    n, d = x.shape
    width = w_in.shape[-1]
    bsz = n // seq
    tm = 1024
    per_seq = seq // tm
    tn = 512
    col_chunks = tuple((lo, min(lo + tn, width)) for lo in range(0, width, tn))

    def mod_idx(k):
        return lambda i: ((layer * bsz + i // per_seq) * N_MOD + k, 0, 0)

    return pl.pallas_call(
        functools.partial(_norm_proj_kernel, sub=256, col_chunks=col_chunks),
        grid=(n // tm,),
        in_specs=[
            pl.BlockSpec((tm, d), lambda i: (i, 0)),
            pl.BlockSpec((None, 1, d), lambda i: (layer, 0, 0)),
            pl.BlockSpec((None, 1, d), mod_idx(1)),
            pl.BlockSpec((None, 1, d), mod_idx(0)),
            pl.BlockSpec((None, d, width), lambda i: (layer, 0, 0)),
        ],
        out_specs=pl.BlockSpec((tm, width), lambda i: (i, 0)),
        out_shape=jax.ShapeDtypeStruct((n, width), BF16),
        compiler_params=_params(("parallel",)),
        name="norm_proj",
    )(x, norm_g, mod, mod, w_in)


ZERO_WEIGHT_LOG2 = -140.0


def _attn_kernel(q_ref, k_ref, v_ref, o_ref, qq_ref, carry_ref, acc_ref, nz_ref, w_ref,
                 more_ref, *, blk, pairs):
    win = blk
    half = blk // 2
    n_blocks = q_ref.shape[0] // blk
    head0 = lax.broadcasted_iota(jnp.int32, (half, LANES), 1) < HEAD_DIM
    qscale = -(HEAD_DIM ** -0.5) * LOG2E
    k_max = jnp.max(jnp.max(jnp.abs(k_ref[...]), axis=0, keepdims=True).astype(F32))

    r = lax.broadcasted_iota(jnp.int32, (win, win), 0)
    c = lax.broadcasted_iota(jnp.int32, (win, win), 1)
    cum = jnp.where(r >= c, 1.0, 0.0).astype(BF16)

    t_lo = lax.broadcasted_iota(jnp.int32, (blk, half), 0) & (half - 1)
    causal_lo = lax.broadcasted_iota(jnp.int32, (blk, half), 1) < t_lo
    t_hi = half + (lax.broadcasted_iota(jnp.int32, (blk, win), 0) & (half - 1))
    causal_hi = lax.broadcasted_iota(jnp.int32, (blk, win), 1) < t_hi
    ps = range(pairs)

    def key_start(w):
        return pl.multiple_of(jnp.maximum(w, 0) * win, win)

    def last_weights_times_v(last, slot, buf):
        start = key_start(last)
        for p in ps:
            acc_ref[buf, p] += jnp.dot(
                w_ref[slot, p], v_ref[pl.ds(start, win), p * LANES:(p + 1) * LANES],
                preferred_element_type=F32)

    def write_output(block, buf):
        out_rows = pl.ds(pl.multiple_of(block * blk, blk), blk)
        for p in ps:
            acc = acc_ref[buf, p]
            o_ref[out_rows, p * LANES:(p + 1) * LANES] = jnp.concatenate(
                [jnp.where(head0, acc[:half], acc[half:blk]),
                 jnp.where(head0, acc[blk:blk + half], acc[blk + half:])], axis=0).astype(BF16)

    def query_block(first, buf, pending):
        q_rows = pl.ds(pl.multiple_of(first * blk, blk), blk)
        q_max = None
        for p in ps:
            q = q_ref[q_rows, p * LANES:(p + 1) * LANES].astype(F32) * qscale
            qq_ref[p] = jnp.concatenate(
                [jnp.where(head0, q[:half], 0.0), jnp.where(head0, 0.0, q[:half]),
                 jnp.where(head0, q[half:], 0.0), jnp.where(head0, 0.0, q[half:])],
                axis=0).astype(BF16)
            q_abs = jnp.abs(q)
            q_max = q_abs if q_max is None else jnp.maximum(q_max, q_abs)
        stop_below = ZERO_WEIGHT_LOG2 - 2.0 ** -8 * (
            1.02 * HEAD_DIM * jnp.max(q_max) * k_max + 2.0)

        def scores(w, slot, p):
            start = key_start(w)
            nz_ref[slot, p] = lax.dot_general(
                qq_ref[p], k_ref[pl.ds(start, win), p * LANES:(p + 1) * LANES],
                (((1,), (1,)), ((), ())), preferred_element_type=F32)

        def weights_times_v(w, slot, p):
            start = key_start(w)
            acc_ref[buf, p] += jnp.dot(w_ref[slot, p],
                                       v_ref[pl.ds(start, win), p * LANES:(p + 1) * LANES],
                                       preferred_element_type=F32)

        def log_stay_of(nz):
            return jnp.minimum(nz, 0.0) - jnp.log(1.0 + jnp.exp2(-jnp.abs(nz))) * LOG2E

        def cumulative(log_stay, width):
            return jnp.dot(log_stay.astype(BF16), cum[:width, :width],
                           preferred_element_type=F32)

        def step(w, slot):
            incl, carries, top = [], [], None
            for p in ps:
                incl.append(cumulative(log_stay_of(nz_ref[slot, p]), win))
                weights_times_v(w + 1, 1 - slot, p)
                scores(w - 1, 1 - slot, p)
                carries.append(carry_ref[p])
                total = carries[p] + incl[p][:, 0:1]
                carry_ref[p] = total
                top = total if top is None else jnp.maximum(top, total)
            more_ref[0] = (jnp.max(top) > stop_below).astype(jnp.int32)
            for p in ps:
                wt = jnp.exp2(incl[p] + carries[p] - nz_ref[slot, p])
                w_ref[slot, p] = wt.astype(BF16)

        def diagonal_step():
            incl, top = [], None
            for p in ps:
                lo = jnp.where(causal_lo, log_stay_of(nz_ref[0, p, :blk, :half]), 0.0)
                hi = jnp.where(causal_hi, log_stay_of(nz_ref[0, p, blk:, :]), 0.0)
                incl.append((cumulative(lo, half), cumulative(hi, win)))
                scores(first - 1, 1, p)
                total = jnp.concatenate([incl[p][0][:, 0:1], incl[p][1][:, 0:1]], axis=0)
                carry_ref[p] = total
                top = total if top is None else jnp.maximum(top, total)
            more_ref[0] = (jnp.max(top) > stop_below).astype(jnp.int32)
            for p in ps:
                lo = jnp.where(causal_lo, jnp.exp2(incl[p][0] - nz_ref[0, p, :blk, :half]), 0.0)
                hi = jnp.where(causal_hi, jnp.exp2(incl[p][1] - nz_ref[0, p, blk:, :]), 0.0)
                w_ref[0, p, :blk, :half] = lo.astype(BF16)
                w_ref[0, p, :blk, half:] = jnp.zeros((blk, half), BF16)
                w_ref[0, p, blk:, :] = hi.astype(BF16)

        def unfinished():
            return more_ref[0] == 1

        prev_block, prev_last, prev_slot = pending
        acc_ref[buf] = jnp.zeros(acc_ref.shape[1:], F32)
        for p in ps:
            scores(first, 0, p)
        last_weights_times_v(prev_last, prev_slot, 1 - buf)
        diagonal_step()
        write_output(prev_block, 1 - buf)

        def body(state):
            w, _, _ = state
            step(w, 1)
            go_second = unfinished() & (w >= 1)

            @pl.when(go_second)
            def _():
                step(w - 1, 0)

            go_on = go_second & (w >= 2) & unfinished()
            return w - 2, go_on, jnp.where(go_second, w - 1, w)

        _, _, last = lax.while_loop(lambda state: state[1], body,
                                    (first - 1, (first >= 1) & unfinished(), first))
        return first, last, (first - last) & 1

    w_ref[...] = jnp.zeros_like(w_ref)
    acc_ref[...] = jnp.zeros_like(acc_ref)

    def two_blocks(m, pending):
        pending = query_block(2 * m, 0, pending)
        return query_block(2 * m + 1, 1, pending)

    zero = jnp.zeros((), jnp.int32)
    block, last, slot = lax.fori_loop(0, n_blocks // 2, two_blocks, (zero, zero, zero))
    last_weights_times_v(last, slot, 1)
    write_output(block, 1)


def _attention(proj, bsz, seq):
    width = proj.shape[-1]
    proj3 = proj.reshape(bsz, seq, width)
    blk = 256
    pairs = SB_WIDTH // LANES

    def column_block(k):
        return pl.BlockSpec((None, seq, SB_WIDTH), lambda b: (b, 0, k))

    return pl.pallas_call(
        functools.partial(_attn_kernel, blk=blk, pairs=pairs),
        grid=(bsz,),
        in_specs=[column_block(0), column_block(1), column_block(2)],
        out_specs=pl.BlockSpec((None, seq, SB_WIDTH), lambda b: (b, 0, 0)),
        out_shape=jax.ShapeDtypeStruct((bsz, seq, SB_WIDTH), BF16),
        scratch_shapes=[pltpu.VMEM((pairs, 2 * blk, LANES), BF16),
                        pltpu.VMEM((pairs, 2 * blk, 1), F32),
                        pltpu.VMEM((2, pairs, 2 * blk, LANES), F32),
                        pltpu.VMEM((2, pairs, 2 * blk, blk), F32),
                        pltpu.VMEM((2, pairs, 2 * blk, blk), BF16),
                        pltpu.SMEM((1,), jnp.int32)],
        compiler_params=_params(("parallel",)),
        name="sb_attention",
    )(proj3, proj3, proj3).reshape(bsz * seq, SB_WIDTH)


def _gelu(x):
    return 0.5 * x * (1.0 + jnp.tanh(0.7978845608028654 * (x + 0.044715 * (x * x * x))))


def _mix_mlp_kernel(a_ref, bg_ref, cg_ref, hc_ref, us_ref, vs_ref, cgh_ref, hch_ref,
                    x_ref, gm_ref, cw_ref, cb_ref, ng_ref, sw_ref, sbias_ref, wo_hbm,
                    g_ref, sc_ref, sh_ref, gf_ref, w1_hbm, w2_hbm, fg_ref,
                    o_ref, uext_ref, cat_ref, wo_ref, w1_ref, w2_ref, stage1_ref, stage2_ref,
                    sem_ref, *,
                    layer, tm, per_seq, halo, tf, final_norm):
    i = pl.program_id(0)
    d, d_ff = w1_ref.shape
    n_chunks = d_ff // tf

    def w1_copy(f):
        return pltpu.make_async_copy(w1_hbm.at[layer, :, f * tf:(f + 1) * tf],
                                     stage1_ref.at[f % 2], sem_ref.at[0, f % 2])

    def w2_copy(f):
        return pltpu.make_async_copy(w2_hbm.at[layer, f * tf:(f + 1) * tf, :],
                                     stage2_ref.at[f % 2], sem_ref.at[1, f % 2])

    def wo_copy(r):
        return pltpu.make_async_copy(wo_hbm.at[layer, r * tf:(r + 1) * tf, :],
                                     stage2_ref.at[r], sem_ref.at[1, r])

    def mixers():
        u = cg_ref[...].astype(F32) * hc_ref[...].astype(F32)
        u_halo = cgh_ref[...].astype(F32) * hch_ref[...].astype(F32)
        u_halo = jnp.where(i % per_seq == 0, 0.0, u_halo)
        uext_ref[0:halo, :] = u_halo
        uext_ref[halo:halo + tm, :] = u
        u_m1 = uext_ref[halo - 1:halo - 1 + tm, :]
        u_m2 = uext_ref[halo - 2:halo - 2 + tm, :]
        cw = cw_ref[...]
        y = cw[0:1] * u_m2 + cw[1:2] * u_m1 + cw[2:3] * u + cb_ref[...]
        c_out = bg_ref[...].astype(F32) * y

        ug = _gelu(us_ref[...].astype(F32))
        vg = _gelu(vs_ref[...].astype(F32))
        vn = vg * lax.rsqrt(jnp.mean(vg * vg, axis=-1, keepdims=True) + EPS) * ng_ref[...]
        t_chunk = lax.broadcasted_iota(jnp.int32, (SG_CHUNK, SG_CHUNK), 0) // CHUNK
        s_chunk = lax.broadcasted_iota(jnp.int32, (SG_CHUNK, SG_CHUNK), 1) // CHUNK
        chunk_causal = t_chunk >= s_chunk
        w_cat = jnp.concatenate(
            [jnp.where(chunk_causal, sw_ref[h], 0.0) for h in range(SG_HEADS)],
            axis=1).astype(BF16)
        lane_head = lax.broadcasted_iota(jnp.int32, (SG_CHUNK, SG_WIDTH), 1) // HEAD_DIM
        sbias = sbias_ref[...]
        mixed = []
        for n in range(tm // SG_CHUNK):
            v_win = vn[n * SG_CHUNK:(n + 1) * SG_CHUNK]
            v_stack = jnp.concatenate(
                [jnp.where(lane_head == h, v_win, 0.0) for h in range(SG_HEADS)],
                axis=0).astype(BF16)
            mixed.append(jnp.dot(w_cat, v_stack, preferred_element_type=F32) + sbias)
        s_out = ug * jnp.concatenate(mixed, axis=0)
        cat_ref[...] = jnp.concatenate(
            [a_ref[...], c_out.astype(BF16), s_out.astype(BF16)], axis=1)

    def tile(load_weights):
        if load_weights:
            for r in range(2):
                wo_copy(r).start()
                w1_copy(r).start()
        cat = cat_ref[...]
        if load_weights:
            for r in range(2):
                wo_copy(r).wait()
                wo_ref[r * tf:(r + 1) * tf, :] = stage2_ref[r].astype(BF16)
                w2_copy(r).start()
        mix = jnp.dot(cat, wo_ref[...], preferred_element_type=F32)
        x = x_ref[...] + gm_ref[...] * mix

        h = _modulated_norm(x, g_ref[...], sc_ref[...], sh_ref[...]).astype(BF16)
        acc = None
        for f in range(n_chunks):
            cols = slice(f * tf, (f + 1) * tf)
            if load_weights:
                w1_copy(f).wait()
                w1_ref[:, cols] = stage1_ref[f % 2].astype(BF16)
                w2_copy(f).wait()
                w2_ref[cols, :] = stage2_ref[f % 2].astype(BF16)
                if f + 2 < n_chunks:
                    w1_copy(f + 2).start()
                    w2_copy(f + 2).start()
            t = jnp.dot(h, w1_ref[:, cols], preferred_element_type=F32)
            t = jnp.square(jnp.maximum(t, 0.0)).astype(BF16)
            part = jnp.dot(t, w2_ref[cols, :], preferred_element_type=F32)
            acc = part if acc is None else acc + part
            if f == 1:
                mixers()
        y = x + gf_ref[...] * acc
        if final_norm:
            y = y * lax.rsqrt(jnp.mean(y * y, axis=-1, keepdims=True) + EPS) * fg_ref[...]
        o_ref[...] = y

    @pl.when(i == 0)
    def _():
        mixers()

    @pl.when(i == 1)
    def _():
        tile(load_weights=True)

    @pl.when(i > 1)
    def _():
        tile(load_weights=False)


def _mix_mlp(a_out, proj, x, mod, conv_w, conv_b, gmlp_norm_g, spatial_w, sbias, w_out,
             norm_g, w1, w2, final_g, layer, seq, final_norm):
    n, d = x.shape
    d_ff = w1.shape[-1]
    bsz = n // seq
    tm = 512
    tf = 512
    per_seq = seq // tm
    halo = BF16_SUBLANES
    col0 = 3 * SB_WIDTH // SC_WIDTH
    resident = pl.Buffered(1)

    tiles = n // tm

    def mixed(i):
        return jnp.minimum(i, tiles - 1)

    def finished(i):
        return jnp.maximum(i - 1, 0)

    def col(k):
        return pl.BlockSpec((tm, SC_WIDTH), lambda i: (mixed(i), col0 + k))

    def halo_col(k):
        return pl.BlockSpec((halo, SC_WIDTH),
                            lambda i: (jnp.maximum(mixed(i) * (tm // halo) - 1, 0), col0 + k))

    def mod_row(k):
        return pl.BlockSpec(
            (None, 1, d),
            lambda i: ((layer * bsz + finished(i) // per_seq) * N_MOD + k, 0, 0))

    def layer_block(*shape):
        zeros = (0,) * len(shape)
        return pl.BlockSpec((None,) + shape, lambda i: (layer,) + zeros, pipeline_mode=resident)

    in_hbm = pl.BlockSpec(memory_space=pl.ANY)
    assert d == 2 * tf and d_ff % tf == 0

    return pl.pallas_call(
        functools.partial(_mix_mlp_kernel, layer=layer, tm=tm, per_seq=per_seq, halo=halo,
                          tf=tf, final_norm=final_norm),
        grid=(tiles + 1,),
        in_specs=[
            pl.BlockSpec((tm, SB_WIDTH), lambda i: (mixed(i), 0)),
            col(0), col(1), col(2), col(3), col(4),
            halo_col(1), halo_col(2),
            pl.BlockSpec((tm, d), lambda i: (finished(i), 0)),
            mod_row(2),
            layer_block(3, SC_WIDTH),
            layer_block(1, SC_WIDTH),
            layer_block(1, SG_WIDTH),
            layer_block(SG_HEADS, SG_CHUNK, SG_CHUNK),
            layer_block(SG_CHUNK, SG_WIDTH),
            in_hbm,
            layer_block(1, d),
            mod_row(4), mod_row(3), mod_row(5),
            in_hbm,
            in_hbm,
            pl.BlockSpec((1, d), lambda i: (0, 0)),
        ],
        out_specs=pl.BlockSpec((tm, d), lambda i: (finished(i), 0)),
        out_shape=jax.ShapeDtypeStruct((n, d), F32),
        scratch_shapes=[pltpu.VMEM((halo + tm, SC_WIDTH), F32),
                        pltpu.VMEM((tm, d), BF16),
                        pltpu.VMEM((d, d), BF16),
                        pltpu.VMEM((d, d_ff), BF16),
                        pltpu.VMEM((d_ff, d), BF16),
                        pltpu.VMEM((2, d, tf), F32),
                        pltpu.VMEM((2, tf, d), F32),
                        pltpu.SemaphoreType.DMA((2, 2))],
        compiler_params=_params(("arbitrary",)),
        name="mix_mlp",
    )(a_out, proj, proj, proj, proj, proj, proj, proj, x, mod,
      conv_w, conv_b, gmlp_norm_g, spatial_w, sbias, w_out,
      norm_g, mod, mod, mod, w1, w2, final_g)


def kernel(x, c, ada_w, ada_b, norm_mix_g, norm_mlp_g, w_in, conv_w, conv_b, gmlp_norm_g,
           spatial_w, spatial_b, w_out, mlp_w1, mlp_w2, final_norm_g):
    bsz, seq, d = x.shape
    depth = ada_w.shape[0]
    xf = x.reshape(bsz * seq, d)

    mod = _adaln(c, ada_w, ada_b).reshape(depth * bsz * N_MOD, 1, d)
    w_in_b = w_in.astype(BF16)
    norm_mix = norm_mix_g.reshape(depth, 1, d)
    norm_mlp = norm_mlp_g.reshape(depth, 1, d)
    conv_b3 = conv_b.reshape(depth, 1, SC_WIDTH)
    gmlp_g3 = gmlp_norm_g.reshape(depth, 1, SG_WIDTH)
    sbias = jnp.repeat(jnp.swapaxes(spatial_b, 1, 2), HEAD_DIM, axis=2)
    final_g = final_norm_g.reshape(1, d)

    for layer in range(depth):
        proj = _norm_proj(xf, norm_mix, mod, w_in_b, layer, seq)
        a_out = _attention(proj, bsz, seq)
        xf = _mix_mlp(a_out, proj, xf, mod, conv_w, conv_b3, gmlp_g3, spatial_w, sbias,
                      w_out, norm_mlp, mlp_w1, mlp_w2, final_g, layer, seq,
                      final_norm=(layer == depth - 1))
    return xf.reshape(bsz, seq, d)
```

```python
import functools

import jax
import jax.numpy as jnp
from jax import lax
from jax.experimental import pallas as pl
from jax.experimental.pallas import tpu as pltpu

F32 = jnp.float32
BF16 = jnp.bfloat16

HEAD_DIM = 64
SB_WIDTH = 512
SC_WIDTH = 256
SG_WIDTH = 256
SG_HEADS = 4
SG_CHUNK = 128
CHUNK = 64
N_MOD = 6
EPS = 1e-6
LANES = 128
BF16_SUBLANES = 16
LOG2E = 1.4426950408889634

VMEM_LIMIT = 48 * 1024 * 1024


def _params(sem):
    return pltpu.CompilerParams(dimension_semantics=sem, vmem_limit_bytes=VMEM_LIMIT)


def _adaln_kernel(c_ref, w_ref, b_ref, o_ref):
    c = c_ref[...]
    c_act = (c * jax.nn.sigmoid(c)).astype(BF16)
    o_ref[...] = jnp.dot(c_act, w_ref[...].astype(BF16), preferred_element_type=F32) + b_ref[...]


def _adaln(c, ada_w, ada_b):
    depth, d, width = ada_w.shape
    bsz = c.shape[0]
    tn = 1536
    return pl.pallas_call(
        _adaln_kernel,
        grid=(depth, width // tn),
        in_specs=[
            pl.BlockSpec((bsz, d), lambda l, j: (0, 0)),
            pl.BlockSpec((None, d, tn), lambda l, j: (l, 0, j)),
            pl.BlockSpec((None, 1, tn), lambda l, j: (l, 0, j)),
        ],
        out_specs=pl.BlockSpec((None, bsz, tn), lambda l, j: (l, 0, j)),
        out_shape=jax.ShapeDtypeStruct((depth, bsz, width), F32),
        compiler_params=_params(("parallel", "parallel")),
        name="adaln",
    )(c, ada_w, ada_b.reshape(depth, 1, width))


def _modulated_norm(x, g, sc, sh):
    r = lax.rsqrt(jnp.mean(x * x, axis=-1, keepdims=True) + EPS)
    return x * r * (g * (1.0 + sc)) + sh


def _norm_proj_kernel(x_ref, g_ref, sc_ref, sh_ref, w_ref, o_ref, *, sub, col_chunks):
    hs = [_modulated_norm(x_ref[r:r + sub, :], g_ref[...], sc_ref[...], sh_ref[...]).astype(BF16)
          for r in range(0, x_ref.shape[0], sub)]
    for k, h in enumerate(hs):
        for lo, hi in col_chunks:
            o_ref[k * sub:(k + 1) * sub, lo:hi] = jnp.dot(
                h, w_ref[:, lo:hi], preferred_element_type=F32).astype(BF16)


def _norm_proj(x, norm_g, mod, w_in, layer, seq):
    n, d = x.shape
    width = w_in.shape[-1]
    bsz = n // seq
    tm = 1024
    per_seq = seq // tm
    tn = 512
    col_chunks = tuple((lo, min(lo + tn, width)) for lo in range(0, width, tn))

    def mod_idx(k):
        return lambda i: ((layer * bsz + i // per_seq) * N_MOD + k, 0, 0)

    return pl.pallas_call(
        functools.partial(_norm_proj_kernel, sub=256, col_chunks=col_chunks),
        grid=(n // tm,),
        in_specs=[
            pl.BlockSpec((tm, d), lambda i: (i, 0)),
            pl.BlockSpec((None, 1, d), lambda i: (layer, 0, 0)),
            pl.BlockSpec((None, 1, d), mod_idx(1)),
            pl.BlockSpec((None, 1, d), mod_idx(0)),
            pl.BlockSpec((None, d, width), lambda i: (layer, 0, 0)),
        ],
        out_specs=pl.BlockSpec((tm, width), lambda i: (i, 0)),
        out_shape=jax.ShapeDtypeStruct((n, width), BF16),
        compiler_params=_params(("parallel",)),
        name="norm_proj",
    )(x, norm_g, mod, mod, w_in)


ZERO_WEIGHT_LOG2 = -140.0


def _attn_kernel(q_ref, k_ref, v_ref, o_ref, qq_ref, carry_ref, acc_ref, nz_ref, w_ref,
                 more_ref, *, blk, pairs):
    win = blk
    half = blk // 2
    n_blocks = q_ref.shape[0] // blk
    head0 = lax.broadcasted_iota(jnp.int32, (half, LANES), 1) < HEAD_DIM
    qscale = -(HEAD_DIM ** -0.5) * LOG2E
    k_max = jnp.max(jnp.max(jnp.abs(k_ref[...]), axis=0, keepdims=True).astype(F32))

    r = lax.broadcasted_iota(jnp.int32, (win, win), 0)
    c = lax.broadcasted_iota(jnp.int32, (win, win), 1)
    cum = jnp.where(r >= c, 1.0, 0.0).astype(BF16)

    t_lo = lax.broadcasted_iota(jnp.int32, (blk, half), 0) & (half - 1)
    causal_lo = lax.broadcasted_iota(jnp.int32, (blk, half), 1) < t_lo
    t_hi = half + (lax.broadcasted_iota(jnp.int32, (blk, win), 0) & (half - 1))
    causal_hi = lax.broadcasted_iota(jnp.int32, (blk, win), 1) < t_hi
    ps = range(pairs)

    def key_start(w):
        return pl.multiple_of(jnp.maximum(w, 0) * win, win)

    def last_weights_times_v(last, slot, buf):
        start = key_start(last)
        for p in ps:
            acc_ref[buf, p] += jnp.dot(
                w_ref[slot, p], v_ref[pl.ds(start, win), p * LANES:(p + 1) * LANES],
                preferred_element_type=F32)

    def write_output(block, buf):
        out_rows = pl.ds(pl.multiple_of(block * blk, blk), blk)
        for p in ps:
            acc = acc_ref[buf, p]
            o_ref[out_rows, p * LANES:(p + 1) * LANES] = jnp.concatenate(
                [jnp.where(head0, acc[:half], acc[half:blk]),
                 jnp.where(head0, acc[blk:blk + half], acc[blk + half:])], axis=0).astype(BF16)

    def query_block(first, buf, pending):
        q_rows = pl.ds(pl.multiple_of(first * blk, blk), blk)
        q_max = None
        for p in ps:
            q = q_ref[q_rows, p * LANES:(p + 1) * LANES].astype(F32) * qscale
            qq_ref[p] = jnp.concatenate(
                [jnp.where(head0, q[:half], 0.0), jnp.where(head0, 0.0, q[:half]),
                 jnp.where(head0, q[half:], 0.0), jnp.where(head0, 0.0, q[half:])],
                axis=0).astype(BF16)
            q_abs = jnp.abs(q)
            q_max = q_abs if q_max is None else jnp.maximum(q_max, q_abs)
        stop_below = ZERO_WEIGHT_LOG2 - 2.0 ** -8 * (
            1.02 * HEAD_DIM * jnp.max(q_max) * k_max + 2.0)

        def scores(w, slot, p):
            start = key_start(w)
            nz_ref[slot, p] = lax.dot_general(
                qq_ref[p], k_ref[pl.ds(start, win), p * LANES:(p + 1) * LANES],
                (((1,), (1,)), ((), ())), preferred_element_type=F32)

        def weights_times_v(w, slot, p):
            start = key_start(w)
            acc_ref[buf, p] += jnp.dot(w_ref[slot, p],
                                       v_ref[pl.ds(start, win), p * LANES:(p + 1) * LANES],
                                       preferred_element_type=F32)

        def log_stay_of(nz):
            return jnp.minimum(nz, 0.0) - jnp.log(1.0 + jnp.exp2(-jnp.abs(nz))) * LOG2E

        def cumulative(log_stay, width):
            return jnp.dot(log_stay.astype(BF16), cum[:width, :width],
                           preferred_element_type=F32)

        def step(w, slot):
            incl, carries, top = [], [], None
            for p in ps:
                incl.append(cumulative(log_stay_of(nz_ref[slot, p]), win))
                weights_times_v(w + 1, 1 - slot, p)
                scores(w - 1, 1 - slot, p)
                carries.append(carry_ref[p])
                total = carries[p] + incl[p][:, 0:1]
                carry_ref[p] = total
                top = total if top is None else jnp.maximum(top, total)
            more_ref[0] = (jnp.max(top) > stop_below).astype(jnp.int32)
            for p in ps:
                wt = jnp.exp2(incl[p] + carries[p] - nz_ref[slot, p])
                w_ref[slot, p] = wt.astype(BF16)

        def diagonal_step():
            incl, top = [], None
            for p in ps:
                lo = jnp.where(causal_lo, log_stay_of(nz_ref[0, p, :blk, :half]), 0.0)
                hi = jnp.where(causal_hi, log_stay_of(nz_ref[0, p, blk:, :]), 0.0)
                incl.append((cumulative(lo, half), cumulative(hi, win)))
                scores(first - 1, 1, p)
                total = jnp.concatenate([incl[p][0][:, 0:1], incl[p][1][:, 0:1]], axis=0)
                carry_ref[p] = total
                top = total if top is None else jnp.maximum(top, total)
            more_ref[0] = (jnp.max(top) > stop_below).astype(jnp.int32)
            for p in ps:
                lo = jnp.where(causal_lo, jnp.exp2(incl[p][0] - nz_ref[0, p, :blk, :half]), 0.0)
                hi = jnp.where(causal_hi, jnp.exp2(incl[p][1] - nz_ref[0, p, blk:, :]), 0.0)
                w_ref[0, p, :blk, :half] = lo.astype(BF16)
                w_ref[0, p, :blk, half:] = jnp.zeros((blk, half), BF16)
                w_ref[0, p, blk:, :] = hi.astype(BF16)

        def unfinished():
            return more_ref[0] == 1

        prev_block, prev_last, prev_slot = pending
        acc_ref[buf] = jnp.zeros(acc_ref.shape[1:], F32)
        for p in ps:
            scores(first, 0, p)
        last_weights_times_v(prev_last, prev_slot, 1 - buf)
        diagonal_step()
        write_output(prev_block, 1 - buf)

        def body(state):
            w, _, _ = state
            step(w, 1)
            go_second = unfinished() & (w >= 1)

            @pl.when(go_second)
            def _():
                step(w - 1, 0)

            go_on = go_second & (w >= 2) & unfinished()
            return w - 2, go_on, jnp.where(go_second, w - 1, w)

        _, _, last = lax.while_loop(lambda state: state[1], body,
                                    (first - 1, (first >= 1) & unfinished(), first))
        return first, last, (first - last) & 1

    w_ref[...] = jnp.zeros_like(w_ref)
    acc_ref[...] = jnp.zeros_like(acc_ref)

    def two_blocks(m, pending):
        pending = query_block(2 * m, 0, pending)
        return query_block(2 * m + 1, 1, pending)

    zero = jnp.zeros((), jnp.int32)
    block, last, slot = lax.fori_loop(0, n_blocks // 2, two_blocks, (zero, zero, zero))
    last_weights_times_v(last, slot, 1)
    write_output(block, 1)


def _attention(proj, bsz, seq):
    width = proj.shape[-1]
    proj3 = proj.reshape(bsz, seq, width)
    blk = 256
    pairs = SB_WIDTH // LANES

    def column_block(k):
        return pl.BlockSpec((None, seq, SB_WIDTH), lambda b: (b, 0, k))

    return pl.pallas_call(
        functools.partial(_attn_kernel, blk=blk, pairs=pairs),
        grid=(bsz,),
        in_specs=[column_block(0), column_block(1), column_block(2)],
        out_specs=pl.BlockSpec((None, seq, SB_WIDTH), lambda b: (b, 0, 0)),
        out_shape=jax.ShapeDtypeStruct((bsz, seq, SB_WIDTH), BF16),
        scratch_shapes=[pltpu.VMEM((pairs, 2 * blk, LANES), BF16),
                        pltpu.VMEM((pairs, 2 * blk, 1), F32),
                        pltpu.VMEM((2, pairs, 2 * blk, LANES), F32),
                        pltpu.VMEM((2, pairs, 2 * blk, blk), F32),
                        pltpu.VMEM((2, pairs, 2 * blk, blk), BF16),
                        pltpu.SMEM((1,), jnp.int32)],
        compiler_params=_params(("parallel",)),
        name="sb_attention",
    )(proj3, proj3, proj3).reshape(bsz * seq, SB_WIDTH)


def _gelu(x):
    return 0.5 * x * (1.0 + jnp.tanh(0.7978845608028654 * (x + 0.044715 * (x * x * x))))


def _mix_mlp_kernel(a_ref, bg_ref, cg_ref, hc_ref, us_ref, vs_ref, cgh_ref, hch_ref,
                    x_ref, gm_ref, cw_ref, cb_ref, ng_ref, sw_ref, sbias_ref, wo_hbm,
                    g_ref, sc_ref, sh_ref, gf_ref, w1_hbm, w2_hbm, fg_ref,
                    o_ref, uext_ref, wo_ref, w1_ref, w2_ref, stage1_ref, stage2_ref, sem_ref, *,
                    layer, tm, per_seq, halo, tf, final_norm):
    i = pl.program_id(0)
    d, d_ff = w1_ref.shape
    n_chunks = d_ff // tf

    def w1_copy(f):
        return pltpu.make_async_copy(w1_hbm.at[layer, :, f * tf:(f + 1) * tf],
                                     stage1_ref.at[f % 2], sem_ref.at[0, f % 2])

    def w2_copy(f):
        return pltpu.make_async_copy(w2_hbm.at[layer, f * tf:(f + 1) * tf, :],
                                     stage2_ref.at[f % 2], sem_ref.at[1, f % 2])

    def wo_copy(r):
        return pltpu.make_async_copy(wo_hbm.at[layer, r * tf:(r + 1) * tf, :],
                                     stage2_ref.at[r], sem_ref.at[1, r])

    def tile(load_weights):
        if load_weights:
            for r in range(2):
                wo_copy(r).start()
                w1_copy(r).start()

        u = cg_ref[...].astype(F32) * hc_ref[...].astype(F32)
        u_halo = cgh_ref[...].astype(F32) * hch_ref[...].astype(F32)
        u_halo = jnp.where(i % per_seq == 0, 0.0, u_halo)
        uext_ref[0:halo, :] = u_halo
        uext_ref[halo:halo + tm, :] = u
        u_m1 = uext_ref[halo - 1:halo - 1 + tm, :]
        u_m2 = uext_ref[halo - 2:halo - 2 + tm, :]
        cw = cw_ref[...]
        y = cw[0:1] * u_m2 + cw[1:2] * u_m1 + cw[2:3] * u + cb_ref[...]
        c_out = bg_ref[...].astype(F32) * y

        ug = _gelu(us_ref[...].astype(F32))
        vg = _gelu(vs_ref[...].astype(F32))
        vn = vg * lax.rsqrt(jnp.mean(vg * vg, axis=-1, keepdims=True) + EPS) * ng_ref[...]
        t_chunk = lax.broadcasted_iota(jnp.int32, (SG_CHUNK, SG_CHUNK), 0) // CHUNK
        s_chunk = lax.broadcasted_iota(jnp.int32, (SG_CHUNK, SG_CHUNK), 1) // CHUNK
        chunk_causal = t_chunk >= s_chunk
        w_cat = jnp.concatenate(
            [jnp.where(chunk_causal, sw_ref[h], 0.0) for h in range(SG_HEADS)],
            axis=1).astype(BF16)
        lane_head = lax.broadcasted_iota(jnp.int32, (SG_CHUNK, SG_WIDTH), 1) // HEAD_DIM
        sbias = sbias_ref[...]
        mixed = []
        for n in range(tm // SG_CHUNK):
            v_win = vn[n * SG_CHUNK:(n + 1) * SG_CHUNK]
            v_stack = jnp.concatenate(
                [jnp.where(lane_head == h, v_win, 0.0) for h in range(SG_HEADS)],
                axis=0).astype(BF16)
            mixed.append(jnp.dot(w_cat, v_stack, preferred_element_type=F32) + sbias)
        s_out = ug * jnp.concatenate(mixed, axis=0)
        cat = jnp.concatenate([a_ref[...], c_out.astype(BF16), s_out.astype(BF16)], axis=1)

        if load_weights:
            for r in range(2):
                wo_copy(r).wait()
                wo_ref[r * tf:(r + 1) * tf, :] = stage2_ref[r].astype(BF16)
                w2_copy(r).start()
        mix = jnp.dot(cat, wo_ref[...], preferred_element_type=F32)
        x = x_ref[...] + gm_ref[...] * mix

        h = _modulated_norm(x, g_ref[...], sc_ref[...], sh_ref[...]).astype(BF16)
        acc = None
        if not load_weights:
            ts = [jnp.square(jnp.maximum(
                jnp.dot(h, w1_ref[:, f * tf:(f + 1) * tf], preferred_element_type=F32),
                0.0)).astype(BF16) for f in range(n_chunks)]
            acc = jnp.dot(jnp.concatenate(ts, axis=1), w2_ref[...], preferred_element_type=F32)
        for f in range(n_chunks if load_weights else 0):
            cols = slice(f * tf, (f + 1) * tf)
            if load_weights:
                w1_copy(f).wait()
                w1_ref[:, cols] = stage1_ref[f % 2].astype(BF16)
                w2_copy(f).wait()
                w2_ref[cols, :] = stage2_ref[f % 2].astype(BF16)
                if f + 2 < n_chunks:
                    w1_copy(f + 2).start()
                    w2_copy(f + 2).start()
            t = jnp.dot(h, w1_ref[:, cols], preferred_element_type=F32)
            t = jnp.square(jnp.maximum(t, 0.0)).astype(BF16)
            part = jnp.dot(t, w2_ref[cols, :], preferred_element_type=F32)
            acc = part if acc is None else acc + part
        y = x + gf_ref[...] * acc
        if final_norm:
            y = y * lax.rsqrt(jnp.mean(y * y, axis=-1, keepdims=True) + EPS) * fg_ref[...]
        o_ref[...] = y

    @pl.when(i == 0)
    def _():
        tile(load_weights=True)

    @pl.when(i > 0)
    def _():
        tile(load_weights=False)


def _mix_mlp(a_out, proj, x, mod, conv_w, conv_b, gmlp_norm_g, spatial_w, sbias, w_out,
             norm_g, w1, w2, final_g, layer, seq, final_norm):
    n, d = x.shape
    d_ff = w1.shape[-1]
    bsz = n // seq
    tm = 512
    tf = 512
    per_seq = seq // tm
    halo = BF16_SUBLANES
    col0 = 3 * SB_WIDTH // SC_WIDTH
    resident = pl.Buffered(1)

    def col(k):
        return pl.BlockSpec((tm, SC_WIDTH), lambda i: (i, col0 + k))

    def halo_col(k):
        return pl.BlockSpec((halo, SC_WIDTH),
                            lambda i: (jnp.maximum(i * (tm // halo) - 1, 0), col0 + k))

    def mod_row(k):
        return pl.BlockSpec((None, 1, d),
                            lambda i: ((layer * bsz + i // per_seq) * N_MOD + k, 0, 0))

    def layer_block(*shape):
        zeros = (0,) * len(shape)
        return pl.BlockSpec((None,) + shape, lambda i: (layer,) + zeros, pipeline_mode=resident)

    in_hbm = pl.BlockSpec(memory_space=pl.ANY)
    assert d == 2 * tf and d_ff % tf == 0

    return pl.pallas_call(
        functools.partial(_mix_mlp_kernel, layer=layer, tm=tm, per_seq=per_seq, halo=halo,
                          tf=tf, final_norm=final_norm),
        grid=(n // tm,),
        in_specs=[
            pl.BlockSpec((tm, SB_WIDTH), lambda i: (i, 0)),
            col(0), col(1), col(2), col(3), col(4),
            halo_col(1), halo_col(2),
            pl.BlockSpec((tm, d), lambda i: (i, 0)),
            mod_row(2),
            layer_block(3, SC_WIDTH),
            layer_block(1, SC_WIDTH),
            layer_block(1, SG_WIDTH),
            layer_block(SG_HEADS, SG_CHUNK, SG_CHUNK),
            layer_block(SG_CHUNK, SG_WIDTH),
            in_hbm,
            layer_block(1, d),
            mod_row(4), mod_row(3), mod_row(5),
            in_hbm,
            in_hbm,
            pl.BlockSpec((1, d), lambda i: (0, 0)),
        ],
        out_specs=pl.BlockSpec((tm, d), lambda i: (i, 0)),
        out_shape=jax.ShapeDtypeStruct((n, d), F32),
        scratch_shapes=[pltpu.VMEM((halo + tm, SC_WIDTH), F32),
                        pltpu.VMEM((d, d), BF16),
                        pltpu.VMEM((d, d_ff), BF16),
                        pltpu.VMEM((d_ff, d), BF16),
                        pltpu.VMEM((2, d, tf), F32),
                        pltpu.VMEM((2, tf, d), F32),
                        pltpu.SemaphoreType.DMA((2, 2))],
        compiler_params=_params(("arbitrary",)),
        name="mix_mlp",
    )(a_out, proj, proj, proj, proj, proj, proj, proj, x, mod,
      conv_w, conv_b, gmlp_norm_g, spatial_w, sbias, w_out,
      norm_g, mod, mod, mod, w1, w2, final_g)


def kernel(x, c, ada_w, ada_b, norm_mix_g, norm_mlp_g, w_in, conv_w, conv_b, gmlp_norm_g,
           spatial_w, spatial_b, w_out, mlp_w1, mlp_w2, final_norm_g):
    bsz, seq, d = x.shape
    depth = ada_w.shape[0]
    xf = x.reshape(bsz * seq, d)

    mod = _adaln(c, ada_w, ada_b).reshape(depth * bsz * N_MOD, 1, d)
    w_in_b = w_in.astype(BF16)
    norm_mix = norm_mix_g.reshape(depth, 1, d)
    norm_mlp = norm_mlp_g.reshape(depth, 1, d)
    conv_b3 = conv_b.reshape(depth, 1, SC_WIDTH)
    gmlp_g3 = gmlp_norm_g.reshape(depth, 1, SG_WIDTH)
    sbias = jnp.repeat(jnp.swapaxes(spatial_b, 1, 2), HEAD_DIM, axis=2)
    final_g = final_norm_g.reshape(1, d)

    for layer in range(depth):
        proj = _norm_proj(xf, norm_mix, mod, w_in_b, layer, seq)
        a_out = _attention(proj, bsz, seq)
        xf = _mix_mlp(a_out, proj, xf, mod, conv_w, conv_b3, gmlp_g3, spatial_w, sbias,
                      w_out, norm_mlp, mlp_w1, mlp_w2, final_g, layer, seq,
                      final_norm=(layer == depth - 1))
    return xf.reshape(bsz, seq, d)
```
